```python
import math
import jax, jax.numpy as jnp
from jax import lax
import numpy as np

D_MODEL = 2048
BATCH = 1
SEQ = 16384
DEPTH = 2
DEC_BATCH = 8
DEC_SEQ = 2048
PAST_LEN = 128

BLOCK = 128
GM_GROUPS = 8
GM_CH = 128
GM_WIDTH = GM_GROUPS * GM_CH
SWA_HEADS = 8
SWA_KV = 2
HEAD_DIM = 128
WINDOW = 128
DIFF_HEADS = 16
DIFF_DIM = 64
N_EXPERTS = 256
TOP_K = 8
N_GROUPS = 8
TOPK_GROUPS = 4
D_EXPERT = 512
ROUTED_SCALE = 2.5
EPS = 1e-6

N_EVEN = (DEPTH + 1) // 2
N_ODD = DEPTH // 2
IN_EVEN = 2 * GM_WIDTH + (SWA_HEADS + 2 * SWA_KV) * HEAD_DIM
MIX_EVEN = GM_WIDTH + SWA_HEADS * HEAD_DIM
MIX_ODD = DIFF_HEADS * 2 * DIFF_DIM
IN_ODD = 3 * MIX_ODD

kernel_name = 'hybrid_gmlp_swa_diffattn_moe_adaln_encoder'


def _alibi_slopes(n):
    return jnp.asarray(np.array([2.0 ** (-8.0 * (h + 1) / n) for h in range(n)], dtype=np.float32))


def _rmsnorm(x, g):
    xf = x.astype(jnp.float32)
    y = xf * lax.rsqrt(jnp.mean(xf * xf, axis=-1, keepdims=True) + EPS)
    return (y * g.astype(jnp.float32)).astype(x.dtype)


def _modulate(h, shift, scale):
    return h * (1 + scale[:, None, :]) + shift[:, None, :]


def _gmlp_chunked(u, v, ln_g, ln_b, ws, bs):
    Bn, S, _ = u.shape
    nc = S // BLOCK
    u = jax.nn.gelu(u)
    vf = jax.nn.gelu(v).astype(jnp.float32).reshape(Bn, nc, BLOCK, GM_GROUPS, GM_CH)
    mu = jnp.mean(vf, axis=-1, keepdims=True)
    var = jnp.mean(jnp.square(vf - mu), axis=-1, keepdims=True)
    vn = (vf - mu) * lax.rsqrt(var + EPS) * ln_g.astype(jnp.float32).reshape(GM_GROUPS, GM_CH) \
        + ln_b.astype(jnp.float32).reshape(GM_GROUPS, GM_CH)
    vn = vn.astype(u.dtype)
    s = jnp.einsum('gpq,bnqgc->bnpgc', ws, vn) + bs.T[None, None, :, :, None]
    return u * s.reshape(Bn, S, GM_WIDTH).astype(u.dtype)


def _swa_attention(q, k, v, qn_g, kn_g, sink):
    Bn, S, _, _ = q.shape
    nb = S // BLOCK
    G = SWA_HEADS // SWA_KV
    q = _rmsnorm(q, qn_g)
    k = _rmsnorm(k, kn_g)
    pad = ((0, 0), (BLOCK, BLOCK), (0, 0), (0, 0))
    kp = jnp.pad(k, pad).reshape(Bn, nb + 2, BLOCK, SWA_KV, HEAD_DIM)
    vp = jnp.pad(v, pad).reshape(Bn, nb + 2, BLOCK, SWA_KV, HEAD_DIM)
    kw = jnp.concatenate([kp[:, :-2], kp[:, 1:-1], kp[:, 2:]], axis=2)
    vw = jnp.concatenate([vp[:, :-2], vp[:, 1:-1], vp[:, 2:]], axis=2)
    qb = q.reshape(Bn, nb, BLOCK, SWA_KV, G, HEAD_DIM)
    scores = jnp.einsum('bnqkgd,bnskd->bnkgqs', qb, kw,
                        preferred_element_type=jnp.float32) * (HEAD_DIM ** -0.5)
    rel = jnp.arange(BLOCK)[:, None] + BLOCK - jnp.arange(3 * BLOCK)[None, :]
    key_pos = jnp.arange(nb)[:, None] * BLOCK - BLOCK + jnp.arange(3 * BLOCK)[None, :]
    valid = (jnp.abs(rel) <= WINDOW)[None, :, :] & ((key_pos >= 0) & (key_pos < S))[:, None, :]
    slopes = _alibi_slopes(SWA_HEADS).reshape(SWA_KV, G)
    bias = -slopes[:, :, None, None] * jnp.abs(rel).astype(jnp.float32)[None, None]
    scores = jnp.where(valid[None, :, None, None], scores + bias[None, None], -jnp.inf)
    sink_col = jnp.broadcast_to(sink.astype(jnp.float32).reshape(SWA_KV, G, 1, 1),
                                scores.shape[:-1] + (1,))
    p = jax.nn.softmax(jnp.concatenate([scores, sink_col], axis=-1), axis=-1)[..., :-1]
    out = jnp.einsum('bnkgqs,bnskd->bnqkgd', p.astype(v.dtype), vw)
    return out.reshape(Bn, S, SWA_HEADS * HEAD_DIM)


def _diff_attention(q, k, v, qn_g, kn_g, lam_q1, lam_k1, lam_q2, lam_k2, subln_g, lam_init):
    Bn, S = q.shape[0], q.shape[1]
    nb = S // BLOCK
    q = _rmsnorm(q, qn_g)
    k = _rmsnorm(k, kn_g)
    lam = (jnp.exp(jnp.sum(lam_q1.astype(jnp.float32) * lam_k1.astype(jnp.float32)))
           - jnp.exp(jnp.sum(lam_q2.astype(jnp.float32) * lam_k2.astype(jnp.float32))) + lam_init)
    slopes = _alibi_slopes(DIFF_HEADS)
    kpos = jnp.arange(S)
    qb = jnp.moveaxis(q.reshape(Bn, nb, BLOCK, DIFF_HEADS, 2, DIFF_DIM), 1, 0)

    def block(args):
        qi, i = args
        s = jnp.einsum('bqhcd,bshcd->bhcqs', qi, k,
                       preferred_element_type=jnp.float32) * (DIFF_DIM ** -0.5)
        qpos = i * BLOCK + jnp.arange(BLOCK)
        dist = jnp.abs(qpos[:, None] - kpos[None, :]).astype(jnp.float32)
        s = s - slopes[None, :, None, None, None] * dist[None, None, None]
        p = jax.nn.softmax(s, axis=-1)
        a = p[:, :, 0] - lam * p[:, :, 1]
        return jnp.einsum('bhqs,bshe->bqhe', a.astype(v.dtype), v)

    o = lax.map(block, (qb, jnp.arange(nb)))
    o = jnp.moveaxis(o, 0, 1).reshape(Bn, S, DIFF_HEADS, 2 * DIFF_DIM)
    o = _rmsnorm(o, subln_g) * (1.0 - lam_init)
    return o.reshape(Bn, S, MIX_ODD)


def _even_mixer(h, w_in, gm_ln_g, gm_ln_b, gm_ws, gm_bs, qn_g, kn_g, sink, w_out):
    Bn, S, _ = h.shape
    z = h @ w_in
    c1 = GM_WIDTH
    c2 = 2 * GM_WIDTH
    c3 = c2 + SWA_HEADS * HEAD_DIM
    c4 = c3 + SWA_KV * HEAD_DIM
    u, v, q, k, vv = jnp.split(z, [c1, c2, c3, c4], axis=-1)
    a = _gmlp_chunked(u, v, gm_ln_g, gm_ln_b, gm_ws, gm_bs)
    b = _swa_attention(q.reshape(Bn, S, SWA_HEADS, HEAD_DIM),
                       k.reshape(Bn, S, SWA_KV, HEAD_DIM),
                       vv.reshape(Bn, S, SWA_KV, HEAD_DIM), qn_g, kn_g, sink)
    return jnp.concatenate([a, b], axis=-1) @ w_out


def _odd_mixer(h, w_in, qn_g, kn_g, lam_q1, lam_k1, lam_q2, lam_k2, subln_g, w_out, lam_init):
    Bn, S, _ = h.shape
    q, k, v = jnp.split(h @ w_in, 3, axis=-1)
    o = _diff_attention(q.reshape(Bn, S, DIFF_HEADS, 2, DIFF_DIM),
                        k.reshape(Bn, S, DIFF_HEADS, 2, DIFF_DIM),
                        v.reshape(Bn, S, DIFF_HEADS, 2 * DIFF_DIM),
                        qn_g, kn_g, lam_q1, lam_k1, lam_q2, lam_k2, subln_g, lam_init)
    return o @ w_out


def _swiglu(x, w1, w3, w2):
    return (jax.nn.silu(x @ w1) * (x @ w3)) @ w2


def _routed_experts(xt, idx, gates, w1, w3, w2, layer):
    T, D = xt.shape
    TK = T * TOP_K
    flat_e = idx.reshape(TK)
    order = jnp.argsort(flat_e)
    e_sorted = flat_e[order]
    tok_sorted = (order // TOP_K).astype(jnp.int32)
    gate_sorted = gates.reshape(TK)[order]
    counts = jnp.zeros((N_EXPERTS,), jnp.int32).at[flat_e].add(1)
    padded = (counts + BLOCK - 1) // BLOCK * BLOCK
    pad_end = jnp.cumsum(padded)
    pad_start = pad_end - padded
    start = jnp.cumsum(counts) - counts
    dest = pad_start[e_sorted] + jnp.arange(TK, dtype=jnp.int32) - start[e_sorted]
    n_blocks = -(-TK // BLOCK) + N_EXPERTS
    rows = n_blocks * BLOCK
    row_tok = jnp.full((rows,), T, jnp.int32).at[dest].set(tok_sorted)
    row_gate = jnp.zeros((rows,), jnp.float32).at[dest].set(gate_sorted)
    block_e = jnp.minimum(jnp.searchsorted(pad_end, jnp.arange(n_blocks, dtype=jnp.int32) * BLOCK,
                                           side='right'), N_EXPERTS - 1)
    xpad = jnp.concatenate([xt, jnp.zeros((1, D), xt.dtype)], axis=0)

    def step(acc, blk):
        tok, gt, e = blk
        xb = xpad[tok]
        hb = jax.nn.silu(xb @ w1[layer, e]) * (xb @ w3[layer, e])
        yb = (hb @ w2[layer, e]).astype(jnp.float32) * gt[:, None]
        return acc.at[tok].add(yb), None

    acc, _ = lax.scan(step, jnp.zeros((T + 1, D), jnp.float32),
                      (row_tok.reshape(n_blocks, BLOCK), row_gate.reshape(n_blocks, BLOCK), block_e))
    return acc[:T].astype(xt.dtype)


def _moe(h, router_w, router_b, w1, w3, w2, sw1, sw3, sw2, layer):
    Bn, S, D = h.shape
    T = Bn * S
    xt = h.reshape(T, D)
    logits = jnp.einsum('td,de->te', xt, router_w, preferred_element_type=jnp.float32)
    scores = jax.nn.sigmoid(logits)
    biased = scores + router_b.astype(jnp.float32)
    grp_score = jnp.sum(lax.top_k(biased.reshape(T, N_GROUPS, N_EXPERTS // N_GROUPS), 2)[0], axis=-1)
    _, gidx = lax.top_k(grp_score, TOPK_GROUPS)
    gmask = jnp.sum(jax.nn.one_hot(gidx, N_GROUPS, dtype=jnp.float32), axis=1) > 0
    emask = jnp.repeat(gmask, N_EXPERTS // N_GROUPS, axis=1)
    _, idx = lax.top_k(jnp.where(emask, biased, -jnp.inf), TOP_K)
    g = jnp.take_along_axis(scores, idx, axis=1)
    g = g / jnp.sum(g, axis=-1, keepdims=True) * ROUTED_SCALE
    routed = _routed_experts(xt, idx, g, w1, w3, w2, layer)
    shared = _swiglu(xt, sw1, sw3, sw2)
    return (routed + shared).reshape(Bn, S, D)


def _trunk(x, c, p):
    for l in range(DEPTH):
        mod = jnp.einsum('bd,de->be', jax.nn.silu(c), p['w_ada'][l]) + p['b_ada'][l]
        sh1, sc1, g1, sh2, sc2, g2 = jnp.split(mod, 6, axis=-1)
        h = _modulate(_rmsnorm(x, p['norm1_g'][l]), sh1, sc1)
        j = l // 2
        if l % 2 == 0:
            mix = _even_mixer(h, p['w_in_e'][j], p['gm_ln_g'][j], p['gm_ln_b'][j], p['gm_ws'][j],
                              p['gm_bs'][j], p['swa_qn_g'][j], p['swa_kn_g'][j], p['swa_sink'][j],
                              p['w_out_e'][j])
        else:
            lam_init = 0.8 - 0.6 * math.exp(-0.3 * l)
            mix = _odd_mixer(h, p['w_in_o'][j], p['diff_qn_g'][j], p['diff_kn_g'][j],
                             p['lam_q1'][j], p['lam_k1'][j], p['lam_q2'][j], p['lam_k2'][j],
                             p['diff_subln_g'][j], p['w_out_o'][j], lam_init)
        x = x + g1[:, None, :] * mix
        h = _modulate(_rmsnorm(x, p['norm2_g'][l]), sh2, sc2)
        x = x + g2[:, None, :] * _moe(h, p['router_w'][l], p['router_b'][l], p['exp_w1'], p['exp_w3'],
                                      p['exp_w2'], p['sh_w1'][l], p['sh_w3'][l], p['sh_w2'][l], l)
    return x


def setup_inputs(seed: int = 0) -> dict:
    key = jax.random.key(seed)
    ks = iter(jax.random.split(key, 48))
    f32 = jnp.float32

    def nrm(shape, scale):
        return jax.random.normal(next(ks), shape, f32) * scale

    def gain(shape):
        return 1.0 + nrm(shape, 0.02)

    D = D_MODEL
    return {
        'x_prompt': nrm((BATCH, SEQ, D), 1.0),
        'x_sample': nrm((DEC_BATCH, DEC_SEQ, D), 1.0),
        'c_prompt': nrm((BATCH, D), 1.0),
        'c_sample': nrm((DEC_BATCH, D), 1.0),
        'w_ada': nrm((DEPTH, D, 6 * D), 0.5 * D ** -0.5),
        'b_ada': nrm((DEPTH, 6 * D), 0.02),
        'norm1_g': gain((DEPTH, D)),
        'norm2_g': gain((DEPTH, D)),
        'w_in_e': nrm((N_EVEN, D, IN_EVEN), D ** -0.5),
        'gm_ln_g': gain((N_EVEN, GM_WIDTH)),
        'gm_ln_b': nrm((N_EVEN, GM_WIDTH), 0.02),
        'gm_ws': nrm((N_EVEN, GM_GROUPS, BLOCK, BLOCK), 0.5 * BLOCK ** -0.5),
        'gm_bs': gain((N_EVEN, GM_GROUPS, BLOCK)),
        'swa_qn_g': gain((N_EVEN, HEAD_DIM)),
        'swa_kn_g': gain((N_EVEN, HEAD_DIM)),
        'swa_sink': nrm((N_EVEN, SWA_HEADS), 0.5),
        'w_out_e': nrm((N_EVEN, MIX_EVEN, D), MIX_EVEN ** -0.5),
        'w_in_o': nrm((N_ODD, D, IN_ODD), D ** -0.5),
        'diff_qn_g': gain((N_ODD, DIFF_DIM)),
        'diff_kn_g': gain((N_ODD, DIFF_DIM)),
        'lam_q1': nrm((N_ODD, DIFF_DIM), 0.1),
        'lam_k1': nrm((N_ODD, DIFF_DIM), 0.1),
        'lam_q2': nrm((N_ODD, DIFF_DIM), 0.1),
        'lam_k2': nrm((N_ODD, DIFF_DIM), 0.1),
        'diff_subln_g': gain((N_ODD, 2 * DIFF_DIM)),
        'w_out_o': nrm((N_ODD, MIX_ODD, D), MIX_ODD ** -0.5),
        'router_w': nrm((DEPTH, D, N_EXPERTS), D ** -0.5),
        'router_b': nrm((DEPTH, N_EXPERTS), 0.01),
        'exp_w1': nrm((DEPTH, N_EXPERTS, D, D_EXPERT), D ** -0.5),
        'exp_w3': nrm((DEPTH, N_EXPERTS, D, D_EXPERT), D ** -0.5),
        'exp_w2': nrm((DEPTH, N_EXPERTS, D_EXPERT, D), D_EXPERT ** -0.5),
        'sh_w1': nrm((DEPTH, D, D_EXPERT), D ** -0.5),
        'sh_w3': nrm((DEPTH, D, D_EXPERT), D ** -0.5),
        'sh_w2': nrm((DEPTH, D_EXPERT, D), D_EXPERT ** -0.5),
    }


def reference(x_prompt, x_sample, c_prompt, c_sample, w_ada, b_ada, norm1_g, norm2_g,
              w_in_e, gm_ln_g, gm_ln_b, gm_ws, gm_bs, swa_qn_g, swa_kn_g, swa_sink, w_out_e,
              w_in_o, diff_qn_g, diff_kn_g, lam_q1, lam_k1, lam_q2, lam_k2, diff_subln_g, w_out_o,
              router_w, router_b, exp_w1, exp_w3, exp_w2, sh_w1, sh_w3, sh_w2):
    p = dict(w_ada=w_ada, b_ada=b_ada, norm1_g=norm1_g, norm2_g=norm2_g,
             w_in_e=w_in_e, gm_ln_g=gm_ln_g, gm_ln_b=gm_ln_b, gm_ws=gm_ws, gm_bs=gm_bs,
             swa_qn_g=swa_qn_g, swa_kn_g=swa_kn_g, swa_sink=swa_sink, w_out_e=w_out_e,
             w_in_o=w_in_o, diff_qn_g=diff_qn_g, diff_kn_g=diff_kn_g, lam_q1=lam_q1, lam_k1=lam_k1,
             lam_q2=lam_q2, lam_k2=lam_k2, diff_subln_g=diff_subln_g, w_out_o=w_out_o,
             router_w=router_w, router_b=router_b, exp_w1=exp_w1, exp_w3=exp_w3, exp_w2=exp_w2,
             sh_w1=sh_w1, sh_w3=sh_w3, sh_w2=sh_w2)
    y_prompt = _trunk(x_prompt, c_prompt, p)
    y_sample = _trunk(x_sample, c_sample, p)
    return (y_prompt, y_sample)
```

```python
import functools
import math

import numpy as np
import jax
import jax.numpy as jnp
from jax import lax
from jax.experimental import pallas as pl
from jax.experimental.pallas import tpu as pltpu

F32 = jnp.float32
BF16 = jnp.bfloat16

BLOCK = 128
GM_GROUPS = 8
GM_CH = 128
GM_WIDTH = GM_GROUPS * GM_CH
SWA_HEADS = 8
SWA_KV = 2
HEAD_DIM = 128
WINDOW = 128
DIFF_HEADS = 16
DIFF_DIM = 64
N_EXPERTS = 256
TOP_K = 8
N_GROUPS = 8
TOPK_GROUPS = 4
D_EXPERT = 512
ROUTED_SCALE = 2.5
EPS = 1e-6
NEG_BIG = -1e30

VMEM_LIMIT = 56 * 1024 * 1024
MOE_ROWS = 256


def _cparams(sem):
    return pltpu.CompilerParams(dimension_semantics=sem, vmem_limit_bytes=VMEM_LIMIT)


def _alibi_slopes(n):
    return [2.0 ** (-8.0 * (h + 1) / n) for h in range(n)]


def _gelu(x):
    return 0.5 * x * (1.0 + jnp.tanh(0.7978845608028654 * (x + 0.044715 * x * x * x)))


def _silu(x):
    return x * jax.nn.sigmoid(x)


class _Segs:
    def __init__(self, groups):
        self.groups = tuple(groups)
        self.tokens = sum(b * s for b, s in groups)
        self.n_seq = sum(b for b, _ in groups)

    def seq_of_row(self, r):
        out = None
        row0, seq0 = 0, 0
        for b, s in self.groups:
            val = seq0 + (r - row0) // s
            out = val if out is None else jnp.where(r >= row0, val, out)
            row0 += b * s
            seq0 += b
        return out

    def pos_in_seq(self, r):
        pos, length = None, None
        row0 = 0
        for b, s in self.groups:
            p = (r - row0) % s
            pos = p if pos is None else jnp.where(r >= row0, p, pos)
            length = s if length is None else jnp.where(r >= row0, s, length)
            row0 += b * s
        return pos, length


def _ada_kernel(c_ref, w_ref, b_ref, o_ref):
    a = _silu(c_ref[...]).astype(BF16)
    o_ref[...] = jnp.dot(a, w_ref[...].astype(BF16), preferred_element_type=F32) + b_ref[...]


def _ada(c_all, w_ada, b_ada):
    depth, d, n = w_ada.shape
    rows = c_all.shape[0]
    tn = 1024
    return pl.pallas_call(
        _ada_kernel,
        grid=(depth, n // tn),
        in_specs=[pl.BlockSpec((rows, d), lambda l, j: (0, 0)),
                  pl.BlockSpec((None, d, tn), lambda l, j: (l, 0, j)),
                  pl.BlockSpec((None, 1, tn), lambda l, j: (l, 0, j))],
        out_specs=pl.BlockSpec((None, rows, tn), lambda l, j: (l, 0, j)),
        out_shape=jax.ShapeDtypeStruct((depth, rows, n), F32),
        compiler_params=_cparams(("parallel", "parallel")),
        name="ada_modulation",
    )(c_all, w_ada, b_ada.reshape(depth, 1, n))


def _mm_kernel(*refs, norm, resid, emit_h):
    it = iter(refs)
    x_ref = next(it)
    w_ref = next(it)
    if norm:
        g_ref, sh_ref, sc_ref = next(it), next(it), next(it)
    if resid:
        res_ref, gate_ref = next(it), next(it)
    o_ref = next(it)
    h_out_ref = next(it) if emit_h else None
    h_sc = next(it) if norm else None

    if norm:
        @pl.when(pl.program_id(1) == 0)
        def _():
            x = x_ref[...]
            y = x * lax.rsqrt(jnp.mean(x * x, axis=-1, keepdims=True) + EPS)
            h = (y * g_ref[...]) * (1.0 + sc_ref[...]) + sh_ref[...]
            hb = h.astype(BF16)
            h_sc[...] = hb
            if emit_h:
                h_out_ref[...] = hb
        lhs = h_sc[...]
    else:
        lhs = x_ref[...]
    acc = jnp.dot(lhs, w_ref[...], preferred_element_type=F32)
    if resid:
        acc = res_ref[...] + gate_ref[...] * acc
    o_ref[...] = acc.astype(o_ref.dtype)


def _matmul(segs, x, w, *, norm=None, resid=None, emit_h=False, out_dtype=F32, tm=512, tn=512):
    m, k = x.shape
    n = w.shape[1]
    tm = min(tm, m)
    tn = min(tn, n)
    seq = lambda i: segs.seq_of_row(i * tm)
    in_specs = [pl.BlockSpec((tm, k), lambda i, j: (i, 0)),
                pl.BlockSpec((k, tn), lambda i, j: (0, j))]
    args = [x, w]
    if norm is not None:
        g, sh, sc = norm
        in_specs += [pl.BlockSpec((1, k), lambda i, j: (0, 0)),
                     pl.BlockSpec((None, 1, k), lambda i, j: (seq(i), 0, 0)),
                     pl.BlockSpec((None, 1, k), lambda i, j: (seq(i), 0, 0))]
        args += [g.reshape(1, k), sh, sc]
    if resid is not None:
        res, gate = resid
        in_specs += [pl.BlockSpec((tm, tn), lambda i, j: (i, j)),
                     pl.BlockSpec((None, 1, tn), lambda i, j: (seq(i), 0, j))]
        args += [res, gate]
    out_shape = [jax.ShapeDtypeStruct((m, n), out_dtype)]
    out_specs = [pl.BlockSpec((tm, tn), lambda i, j: (i, j))]
    if emit_h:
        out_shape.append(jax.ShapeDtypeStruct((m, k), BF16))
        out_specs.append(pl.BlockSpec((tm, k), lambda i, j: (i, 0)))
    scratch = [pltpu.VMEM((tm, k), BF16)] if norm is not None else []
    outs = pl.pallas_call(
        functools.partial(_mm_kernel, norm=norm is not None, resid=resid is not None, emit_h=emit_h),
        grid=(m // tm, n // tn),
        in_specs=in_specs,
        out_specs=out_specs,
        out_shape=out_shape,
        scratch_shapes=scratch,
        compiler_params=_cparams(("parallel", "arbitrary")),
        name="dense_matmul",
    )(*args)
    return outs if emit_h else outs[0]


def _gmlp_kernel(u_ref, v_ref, lng_ref, lnb_ref, ws_ref, bs_ref, o_ref, *, chunks):
    for c in range(chunks):
        rows = slice(c * BLOCK, (c + 1) * BLOCK)
        for g in range(GM_GROUPS):
            cols = slice(g * GM_CH, (g + 1) * GM_CH)
            v = _gelu(v_ref[rows, cols])
            mu = jnp.mean(v, axis=-1, keepdims=True)
            vc = v - mu
            var = jnp.mean(vc * vc, axis=-1, keepdims=True)
            vn = vc * lax.rsqrt(var + EPS) * lng_ref[:, cols] + lnb_ref[:, cols]
            s = jnp.dot(ws_ref[g], vn.astype(BF16), preferred_element_type=F32) + bs_ref[g]
            o_ref[rows, cols] = (_gelu(u_ref[rows, cols]) * s).astype(o_ref.dtype)


def _gmlp(z, ln_g, ln_b, ws, bs):
    t = z.shape[0]
    rows = min(256, t)
    return pl.pallas_call(
        functools.partial(_gmlp_kernel, chunks=rows // BLOCK),
        grid=(t // rows,),
        in_specs=[pl.BlockSpec((rows, GM_WIDTH), lambda i: (i, 0)),
                  pl.BlockSpec((rows, GM_WIDTH), lambda i: (i, 1)),
                  pl.BlockSpec((1, GM_WIDTH), lambda i: (0, 0)),
                  pl.BlockSpec((1, GM_WIDTH), lambda i: (0, 0)),
                  pl.BlockSpec((GM_GROUPS, BLOCK, BLOCK), lambda i: (0, 0, 0)),
                  pl.BlockSpec((GM_GROUPS, BLOCK, 1), lambda i: (0, 0, 0))],
        out_specs=pl.BlockSpec((rows, GM_WIDTH), lambda i: (i, 0)),
        out_shape=jax.ShapeDtypeStruct((t, GM_WIDTH), BF16),
        compiler_params=_cparams(("parallel",)),
        name="gmlp_mixer",
    )(z, z, ln_g.reshape(1, GM_WIDTH), ln_b.reshape(1, GM_WIDTH), ws.astype(BF16),
      bs.reshape(GM_GROUPS, BLOCK, 1))


def _swa_kernel(sink_ref, q_ref, kp_ref, kc_ref, kn_ref, vp_ref, vc_ref, vn_ref, qg_ref, kg_ref, o_ref,
                *, segs):
    n = pl.program_id(0)
    pos, length = segs.pos_in_seq(n * BLOCK)
    has_prev = pos > 0
    has_next = pos + BLOCK < length

    qi = lax.broadcasted_iota(jnp.int32, (BLOCK, 3 * BLOCK), 0)
    sj = lax.broadcasted_iota(jnp.int32, (BLOCK, 3 * BLOCK), 1)
    rel = jnp.abs(qi + BLOCK - sj)
    valid = rel <= WINDOW
    valid = valid & ((sj >= BLOCK) | has_prev) & ((sj < 2 * BLOCK) | has_next)
    dist = rel.astype(F32)
    slopes = _alibi_slopes(SWA_HEADS)
    group = SWA_HEADS // SWA_KV
    scale = HEAD_DIM ** -0.5

    def rms(x, g):
        return x * lax.rsqrt(jnp.mean(x * x, axis=-1, keepdims=True) + EPS) * g

    kcat = jnp.concatenate([kp_ref[...], kc_ref[...], kn_ref[...]], axis=0)
    vcat = jnp.concatenate([vp_ref[...], vc_ref[...], vn_ref[...]], axis=0).astype(BF16)
    for kh in range(SWA_KV):
        cols = slice(kh * HEAD_DIM, (kh + 1) * HEAD_DIM)
        k = rms(kcat[:, cols], kg_ref[...]).astype(BF16)
        v = vcat[:, cols]
        for gi in range(group):
            h = kh * group + gi
            hc = slice(h * HEAD_DIM, (h + 1) * HEAD_DIM)
            q = (rms(q_ref[:, hc], qg_ref[...]) * scale).astype(BF16)
            s = lax.dot_general(q, k, (((1,), (1,)), ((), ())), preferred_element_type=F32)
            s = jnp.where(valid, s - slopes[h] * dist, NEG_BIG)
            sink = sink_ref[h]
            m = jnp.maximum(jnp.max(s, axis=-1, keepdims=True), sink)
            p = jnp.exp(s - m)
            denom = jnp.sum(p, axis=-1, keepdims=True) + jnp.exp(sink - m)
            o = jnp.dot(p.astype(BF16), v, preferred_element_type=F32)
            o_ref[:, hc] = (o / denom).astype(o_ref.dtype)


def _swa(segs, z, qn_g, kn_g, sink):
    t = z.shape[0]
    nb = t // BLOCK
    qw = SWA_HEADS * HEAD_DIM
    kw = SWA_KV * HEAD_DIM
    q_blk = 2 * GM_WIDTH // qw
    k_blk = (2 * GM_WIDTH + qw) // kw
    v_blk = k_blk + 1
    prev = lambda n: jnp.maximum(n - 1, 0)
    nxt = lambda n: jnp.minimum(n + 1, nb - 1)
    return pl.pallas_call(
        functools.partial(_swa_kernel, segs=segs),
        grid=(nb,),
        in_specs=[pl.BlockSpec(memory_space=pltpu.SMEM),
                  pl.BlockSpec((BLOCK, qw), lambda n: (n, q_blk)),
                  pl.BlockSpec((BLOCK, kw), lambda n: (prev(n), k_blk)),
                  pl.BlockSpec((BLOCK, kw), lambda n: (n, k_blk)),
                  pl.BlockSpec((BLOCK, kw), lambda n: (nxt(n), k_blk)),
                  pl.BlockSpec((BLOCK, kw), lambda n: (prev(n), v_blk)),
                  pl.BlockSpec((BLOCK, kw), lambda n: (n, v_blk)),
                  pl.BlockSpec((BLOCK, kw), lambda n: (nxt(n), v_blk)),
                  pl.BlockSpec((1, HEAD_DIM), lambda n: (0, 0)),
                  pl.BlockSpec((1, HEAD_DIM), lambda n: (0, 0))],
        out_specs=pl.BlockSpec((BLOCK, qw), lambda n: (n, 0)),
        out_shape=jax.ShapeDtypeStruct((t, qw), BF16),
        compiler_params=_cparams(("parallel",)),
        name="swa_mixer",
    )(sink.astype(F32), z, z, z, z, z, z, z, qn_g.reshape(1, HEAD_DIM), kn_g.reshape(1, HEAD_DIM))


def _diff_prep_kernel(z_ref, qg_ref, kg_ref, seg_ref, q_ref, k0_ref, k1_ref, v_ref):
    width = DIFF_HEADS * 2 * DIFF_DIM
    seg = seg_ref[...]
    lane = lax.broadcasted_iota(jnp.int32, (1, 2 * DIFF_DIM), 1)
    first = lane < DIFF_DIM

    def seg_mean_sq(x):
        x2 = x * x
        hi = x2.astype(BF16)
        lo = (x2 - hi.astype(F32)).astype(BF16)
        tot = jnp.dot(hi, seg, preferred_element_type=F32) + jnp.dot(lo, seg, preferred_element_type=F32)
        return tot * (1.0 / DIFF_DIM)

    for h in range(DIFF_HEADS):
        cols = slice(h * 2 * DIFF_DIM, (h + 1) * 2 * DIFF_DIM)
        q = z_ref[:, cols]
        qn = q * lax.rsqrt(seg_mean_sq(q) + EPS) * qg_ref[...] * (DIFF_DIM ** -0.5)
        q_ref[:, cols] = qn.astype(BF16)
        k = z_ref[:, width + h * 2 * DIFF_DIM: width + (h + 1) * 2 * DIFF_DIM]
        kn = k * lax.rsqrt(seg_mean_sq(k) + EPS) * kg_ref[...]
        k0_ref[:, cols] = jnp.where(first, kn, 0.0).astype(BF16)
        k1_ref[:, cols] = jnp.where(first, 0.0, kn).astype(BF16)
    v_ref[...] = z_ref[:, 2 * width:].astype(BF16)


def _diff_prep(z, qn_g, kn_g):
    t = z.shape[0]
    width = DIFF_HEADS * 2 * DIFF_DIM
    tm = min(256, t)
    seg = np.kron(np.eye(2, dtype=np.float32), np.ones((DIFF_DIM, DIFF_DIM), np.float32))
    qg2 = jnp.concatenate([qn_g, qn_g]).reshape(1, 2 * DIFF_DIM)
    kg2 = jnp.concatenate([kn_g, kn_g]).reshape(1, 2 * DIFF_DIM)
    out = jax.ShapeDtypeStruct((t, width), BF16)
    ospec = pl.BlockSpec((tm, width), lambda i: (i, 0))
    return pl.pallas_call(
        _diff_prep_kernel,
        grid=(t // tm,),
        in_specs=[pl.BlockSpec((tm, 3 * width), lambda i: (i, 0)),
                  pl.BlockSpec((1, 2 * DIFF_DIM), lambda i: (0, 0)),
                  pl.BlockSpec((1, 2 * DIFF_DIM), lambda i: (0, 0)),
                  pl.BlockSpec((2 * DIFF_DIM, 2 * DIFF_DIM), lambda i: (0, 0))],
        out_specs=[ospec, ospec, ospec, ospec],
        out_shape=[out, out, out, out],
        compiler_params=_cparams(("parallel",)),
        name="diff_prep",
    )(z, qg2, kg2, jnp.asarray(seg, BF16))


def _diff_kernel(lam_ref, slope_ref, q_ref, k0_ref, k1_ref, v_ref, sg_ref, o_ref, m_sc, l_sc, acc_sc,
                 *, tq, tk, out_scale):
    h = pl.program_id(1)
    i = pl.program_id(2)
    j = pl.program_id(3)

    @pl.when(j == 0)
    def _():
        m_sc[...] = jnp.full_like(m_sc, NEG_BIG)
        l_sc[...] = jnp.zeros_like(l_sc)
        acc_sc[...] = jnp.zeros_like(acc_sc)

    qpos = i * tq + lax.broadcasted_iota(jnp.int32, (tq, tk), 0)
    kpos = j * tk + lax.broadcasted_iota(jnp.int32, (tq, tk), 1)
    bias = slope_ref[h] * jnp.abs(qpos - kpos).astype(F32)
    q = q_ref[...]
    v = v_ref[...]
    for c, k_ref in enumerate((k0_ref, k1_ref)):
        s = lax.dot_general(q, k_ref[...], (((1,), (1,)), ((), ())), preferred_element_type=F32) - bias
        m_prev = m_sc[c]
        m_new = jnp.maximum(m_prev, jnp.max(s, axis=-1, keepdims=True))
        alpha = jnp.exp(m_prev - m_new)
        p = jnp.exp(s - m_new)
        l_sc[c] = alpha * l_sc[c] + jnp.sum(p, axis=-1, keepdims=True)
        acc_sc[c] = alpha * acc_sc[c] + jnp.dot(p.astype(BF16), v, preferred_element_type=F32)
        m_sc[c] = m_new

    @pl.when(j == pl.num_programs(3) - 1)
    def _():
        o = acc_sc[0] / l_sc[0] - lam_ref[0] * (acc_sc[1] / l_sc[1])
        o = o * lax.rsqrt(jnp.mean(o * o, axis=-1, keepdims=True) + EPS) * sg_ref[...] * out_scale
        o_ref[...] = o.astype(o_ref.dtype)


def _diff_attention(segs, q, k0, k1, v, lam, subln_g, lam_init):
    t = q.shape[0]
    hd = 2 * DIFF_DIM
    slopes = jnp.asarray(_alibi_slopes(DIFF_HEADS), F32)
    outs = []
    row0 = 0
    for n_seq, s_len in segs.groups:
        tq = min(512, s_len)
        tk = min(512, s_len)
        qb0 = row0 // tq
        kb0 = row0 // tk
        nq = s_len // tq
        nk = s_len // tk
        kspec = pl.BlockSpec((tk, hd), lambda b, h, i, j: (kb0 + b * nk + j, h))
        out = pl.pallas_call(
            functools.partial(_diff_kernel, tq=tq, tk=tk, out_scale=1.0 - lam_init),
            grid=(n_seq, DIFF_HEADS, nq, nk),
            in_specs=[pl.BlockSpec(memory_space=pltpu.SMEM),
                      pl.BlockSpec(memory_space=pltpu.SMEM),
                      pl.BlockSpec((tq, hd), lambda b, h, i, j: (qb0 + b * nq + i, h)),
                      kspec, kspec, kspec,
                      pl.BlockSpec((1, hd), lambda b, h, i, j: (0, 0))],
            out_specs=pl.BlockSpec((tq, hd), lambda b, h, i, j: (b * nq + i, h)),
            out_shape=jax.ShapeDtypeStruct((n_seq * s_len, DIFF_HEADS * hd), BF16),
            scratch_shapes=[pltpu.VMEM((2, tq, 1), F32), pltpu.VMEM((2, tq, 1), F32),
                            pltpu.VMEM((2, tq, hd), F32)],
            compiler_params=_cparams(("parallel", "parallel", "parallel", "arbitrary")),
            name="diff_attention",
        )(lam, slopes, q, k0, k1, v, subln_g.reshape(1, hd))
        outs.append(out)
        row0 += n_seq * s_len
    return jnp.concatenate(outs, axis=0)


def _expert_kernel(be_ref, first_ref, used_ref, x_ref, w1_ref, w3_ref, w2_ref, o_ref, w1_sc, w3_sc, w2_sc):
    i = pl.program_id(0)

    @pl.when(first_ref[i] == 1)
    def _():
        w1_sc[...] = w1_ref[...].astype(BF16)
        w3_sc[...] = w3_ref[...].astype(BF16)
        w2_sc[...] = w2_ref[...].astype(BF16)

    @pl.when(i < used_ref[0])
    def _():
        x = x_ref[...]
        a = jnp.dot(x, w1_sc[...], preferred_element_type=F32)
        b = jnp.dot(x, w3_sc[...], preferred_element_type=F32)
        hb = (_silu(a) * b).astype(BF16)
        o_ref[...] = jnp.dot(hb, w2_sc[...], preferred_element_type=F32).astype(o_ref.dtype)

    @pl.when(i >= used_ref[0])
    def _():
        o_ref[...] = jnp.zeros_like(o_ref)


def _experts(xs, block_e, first, used, w1, w3, w2, layer):
    rows, d = xs.shape
    tm = MOE_ROWS
    n_blocks = rows // tm
    de = w1.shape[-1]
    grid_spec = pltpu.PrefetchScalarGridSpec(
        num_scalar_prefetch=3,
        grid=(n_blocks,),
        in_specs=[pl.BlockSpec((tm, d), lambda i, be, fi, us: (i, 0)),
                  pl.BlockSpec((None, None, d, de), lambda i, be, fi, us: (layer, be[i], 0, 0)),
                  pl.BlockSpec((None, None, d, de), lambda i, be, fi, us: (layer, be[i], 0, 0)),
                  pl.BlockSpec((None, None, de, d), lambda i, be, fi, us: (layer, be[i], 0, 0))],
        out_specs=pl.BlockSpec((tm, d), lambda i, be, fi, us: (i, 0)),
        scratch_shapes=[pltpu.VMEM((d, de), BF16), pltpu.VMEM((d, de), BF16), pltpu.VMEM((de, d), BF16)],
    )
    return pl.pallas_call(
        _expert_kernel,
        grid_spec=grid_spec,
        out_shape=jax.ShapeDtypeStruct((rows, d), BF16),
        compiler_params=_cparams(("arbitrary",)),
        name="swiglu_experts",
    )(block_e, first, used, xs, w1, w3, w2)


def _route(logits, router_b):
    t = logits.shape[0]
    scores = jax.nn.sigmoid(logits)
    biased = scores + router_b.astype(F32)
    grp = jnp.sum(lax.top_k(biased.reshape(t, N_GROUPS, N_EXPERTS // N_GROUPS), 2)[0], axis=-1)
    _, gidx = lax.top_k(grp, TOPK_GROUPS)
    gmask = jnp.sum(jax.nn.one_hot(gidx, N_GROUPS, dtype=F32), axis=1) > 0
    emask = jnp.repeat(gmask, N_EXPERTS // N_GROUPS, axis=1)
    _, idx = lax.top_k(jnp.where(emask, biased, -jnp.inf), TOP_K)
    g = jnp.take_along_axis(scores, idx, axis=1)
    g = g / jnp.sum(g, axis=-1, keepdims=True) * ROUTED_SCALE
    return idx, g


def _dispatch_plan(idx, tm):
    t = idx.shape[0]
    n_blocks = (t * TOP_K) // tm + N_EXPERTS
    hit = jnp.sum(jax.nn.one_hot(idx, N_EXPERTS, dtype=jnp.int32), axis=1)
    before = jnp.cumsum(hit, axis=0) - hit
    counts = jnp.sum(hit, axis=0)
    rank = jnp.take_along_axis(before, idx, axis=1)
    padded = (counts + tm - 1) // tm * tm
    pad_end = jnp.cumsum(padded)
    pad_start = pad_end - padded
    dest = (pad_start[idx] + rank).astype(jnp.int32)
    tok = jnp.broadcast_to(jnp.arange(t, dtype=jnp.int32)[:, None], (t, TOP_K))
    row_tok = jnp.zeros((n_blocks * tm,), jnp.int32).at[dest.reshape(-1)].set(tok.reshape(-1))
    block_e = jnp.minimum(jnp.searchsorted(pad_end, jnp.arange(n_blocks, dtype=jnp.int32) * tm, side='right'),
                          N_EXPERTS - 1).astype(jnp.int32)
    first = jnp.concatenate([jnp.ones((1,), jnp.int32), (block_e[1:] != block_e[:-1]).astype(jnp.int32)])
    used = (pad_end[-1:] // tm).astype(jnp.int32)
    return dest, row_tok, block_e, first, used


def _combine_kernel(x_ref, gate_ref, ysh_ref, yg_ref, g_ref, o_ref):
    acc = ysh_ref[...].astype(F32)
    for k in range(TOP_K):
        acc = acc + g_ref[:, k:k + 1] * yg_ref[k].astype(F32)
    o_ref[...] = x_ref[...] + gate_ref[...] * acc


def _combine(segs, x, gate, ysh, yg, g):
    t, d = x.shape
    tm = min(256, t)
    seq = lambda i: segs.seq_of_row(i * tm)
    return pl.pallas_call(
        _combine_kernel,
        grid=(t // tm,),
        in_specs=[pl.BlockSpec((tm, d), lambda i: (i, 0)),
                  pl.BlockSpec((None, 1, d), lambda i: (seq(i), 0, 0)),
                  pl.BlockSpec((tm, d), lambda i: (i, 0)),
                  pl.BlockSpec((TOP_K, tm, d), lambda i: (0, i, 0)),
                  pl.BlockSpec((tm, TOP_K), lambda i: (i, 0))],
        out_specs=pl.BlockSpec((tm, d), lambda i: (i, 0)),
        out_shape=jax.ShapeDtypeStruct((t, d), F32),
        compiler_params=_cparams(("parallel",)),
        name="moe_combine",
    )(x, gate, ysh, yg, g)


def _moe_layer(segs, x, norm, gate, router_w, router_b, exp_w1, exp_w3, exp_w2, sh_w1, sh_w3, sh_w2, layer):
    t, d = x.shape
    logits, h = _matmul(segs, x, router_w.astype(BF16), norm=norm, emit_h=True)
    idx, g = _route(logits, router_b)
    dest, row_tok, block_e, first, used = _dispatch_plan(idx, MOE_ROWS)
    xs = jnp.take(h, row_tok, axis=0)
    ys = _experts(xs, block_e, first, used, exp_w1, exp_w3, exp_w2, layer)
    yg = jnp.take(ys, dest.T, axis=0)
    n_sh = t // MOE_ROWS
    zeros = jnp.zeros((n_sh,), jnp.int32)
    ysh = _experts(h, zeros + layer, zeros.at[0].set(1), jnp.full((1,), n_sh, jnp.int32),
                   sh_w1[None], sh_w3[None], sh_w2[None], 0)
    return _combine(segs, x, gate, ysh, yg, g)


def kernel(x_prompt, x_sample, c_prompt, c_sample, w_ada, b_ada, norm1_g, norm2_g, w_in_e, gm_ln_g, gm_ln_b,
           gm_ws, gm_bs, swa_qn_g, swa_kn_g, swa_sink, w_out_e, w_in_o, diff_qn_g, diff_kn_g, lam_q1, lam_k1,
           lam_q2, lam_k2, diff_subln_g, w_out_o, router_w, router_b, exp_w1, exp_w3, exp_w2, sh_w1, sh_w3,
           sh_w2):
    d = x_prompt.shape[-1]
    depth = w_ada.shape[0]
    segs = _Segs([x_prompt.shape[:2], x_sample.shape[:2]])
    n_seq = segs.n_seq
    x = jnp.concatenate([x_prompt.reshape(-1, d), x_sample.reshape(-1, d)], axis=0)

    c_rows = -(-n_seq // 16) * 16
    c_all = jnp.zeros((c_rows, d), F32).at[:n_seq].set(jnp.concatenate([c_prompt, c_sample], axis=0))
    mod = _ada(c_all, w_ada, b_ada)[:, :n_seq]

    for l in range(depth):
        sh1, sc1, g1, sh2, sc2, g2 = [m.reshape(n_seq, 1, d) for m in jnp.split(mod[l], 6, axis=-1)]
        j = l // 2
        if l % 2 == 0:
            z = _matmul(segs, x, w_in_e[j].astype(BF16), norm=(norm1_g[l], sh1, sc1))
            a = _gmlp(z, gm_ln_g[j], gm_ln_b[j], gm_ws[j], gm_bs[j])
            b = _swa(segs, z, swa_qn_g[j], swa_kn_g[j], swa_sink[j])
            mix = jnp.concatenate([a, b], axis=-1)
            x = _matmul(segs, mix, w_out_e[j].astype(BF16), resid=(x, g1))
        else:
            lam_init = 0.8 - 0.6 * math.exp(-0.3 * l)
            z = _matmul(segs, x, w_in_o[j].astype(BF16), norm=(norm1_g[l], sh1, sc1))
            q, k0, k1, v = _diff_prep(z, diff_qn_g[j], diff_kn_g[j])
            lam = (jnp.exp(jnp.sum(lam_q1[j].astype(F32) * lam_k1[j].astype(F32)))
                   - jnp.exp(jnp.sum(lam_q2[j].astype(F32) * lam_k2[j].astype(F32))) + lam_init).reshape(1)
            o = _diff_attention(segs, q, k0, k1, v, lam, diff_subln_g[j], lam_init)
            x = _matmul(segs, o, w_out_o[j].astype(BF16), resid=(x, g1))
        x = _moe_layer(segs, x, (norm2_g[l], sh2, sc2), g2, router_w[l], router_b[l], exp_w1, exp_w3, exp_w2,
                       sh_w1, sh_w3, sh_w2, l)

    t0 = x_prompt.shape[0] * x_prompt.shape[1]
    return x[:t0].reshape(x_prompt.shape), x[t0:].reshape(x_sample.shape)
```

```python
import functools
import math

import ml_dtypes
import numpy as np
import jax
import jax.numpy as jnp
from jax import lax
from jax.experimental import pallas as pl
from jax.experimental.pallas import tpu as pltpu

F32 = jnp.float32
BF16 = jnp.bfloat16
U32 = jnp.uint32
I32 = jnp.int32

BLOCK = 128
GM_GROUPS = 8
GM_CH = 128
GM_WIDTH = GM_GROUPS * GM_CH
SWA_HEADS = 8
SWA_KV = 2
HEAD_DIM = 128
WINDOW = 128
DIFF_HEADS = 16
DIFF_DIM = 64
N_EXPERTS = 256
TOP_K = 8
N_GROUPS = 8
TOPK_GROUPS = 4
D_EXPERT = 512
ROUTED_SCALE = 2.5
EPS = 1e-6
NEG_BIG = -1e30
LOG2E = 1.4426950408889634

LANES = 128
VMEM_LIMIT = 56 * 1024 * 1024
MOE_ROWS = 256
ROUTE_ROWS = 256
MAX_FIXED_SHIFT = 50.0


def _cparams(sem):
    return pltpu.CompilerParams(dimension_semantics=sem, vmem_limit_bytes=VMEM_LIMIT)


def _alibi_slopes(n):
    return [2.0 ** (-8.0 * (h + 1) / n) for h in range(n)]


def _gelu(x):
    return 0.5 * x * (1.0 + jnp.tanh(0.7978845608028654 * (x + 0.044715 * x * x * x)))


def _silu(x):
    return x * jax.nn.sigmoid(x)


class _Segs:
    def __init__(self, groups):
        self.groups = tuple(groups)
        self.tokens = sum(b * s for b, s in groups)
        self.n_seq = sum(b for b, _ in groups)

    def seq_of_row(self, r):
        out = None
        row0, seq0 = 0, 0
        for b, s in self.groups:
            val = seq0 + (r - row0) // s
            out = val if out is None else jnp.where(r >= row0, val, out)
            row0 += b * s
            seq0 += b
        return out

    def pos_in_seq(self, r):
        pos, length = None, None
        row0 = 0
        for b, s in self.groups:
            p = (r - row0) % s
            pos = p if pos is None else jnp.where(r >= row0, p, pos)
            length = s if length is None else jnp.where(r >= row0, s, length)
            row0 += b * s
        return pos, length


def _ada_kernel(c_ref, w_ref, b_ref, o_ref):
    a = _silu(c_ref[...]).astype(BF16)
    o_ref[...] = jnp.dot(a, w_ref[...].astype(BF16), preferred_element_type=F32) + b_ref[...]


def _ada(c_all, w_ada, b_ada):
    depth, d, n = w_ada.shape
    rows = c_all.shape[0]
    tn = 1024
    return pl.pallas_call(
        _ada_kernel,
        grid=(depth, n // tn),
        in_specs=[pl.BlockSpec((rows, d), lambda l, j: (0, 0)),
                  pl.BlockSpec((None, d, tn), lambda l, j: (l, 0, j)),
                  pl.BlockSpec((None, 1, tn), lambda l, j: (l, 0, j))],
        out_specs=pl.BlockSpec((None, rows, tn), lambda l, j: (l, 0, j)),
        out_shape=jax.ShapeDtypeStruct((depth, rows, n), F32),
        compiler_params=_cparams(("parallel", "parallel")),
        name="ada_modulation",
    )(c_all, w_ada, b_ada.reshape(depth, 1, n))


def _norm_modulate(x, g, shift, scale):
    y = x * lax.rsqrt(jnp.mean(x * x, axis=-1, keepdims=True) + EPS)
    return (y * g) * (1.0 + scale) + shift


def _mm_kernel(*refs, norm, resid):
    it = iter(refs)
    x_ref = next(it)
    w_ref = next(it)
    if norm:
        g_ref, sh_ref, sc_ref = next(it), next(it), next(it)
    if resid:
        res_ref, gate_ref = next(it), next(it)
    o_ref = next(it)
    h_sc = next(it) if norm else None

    if norm:
        @pl.when(pl.program_id(1) == 0)
        def _():
            h_sc[...] = _norm_modulate(x_ref[...], g_ref[...], sh_ref[...], sc_ref[...]).astype(BF16)
        lhs = h_sc[...]
    else:
        lhs = x_ref[...]
    acc = jnp.dot(lhs, w_ref[...], preferred_element_type=F32)
    if resid:
        acc = res_ref[...] + gate_ref[...] * acc
    o_ref[...] = acc.astype(o_ref.dtype)


def _matmul(segs, x, w, *, norm=None, resid=None, out_dtype=F32, tm=512, tn=512):
    m, k = x.shape
    n = w.shape[1]
    tm = min(tm, m)
    tn = min(tn, n)
    seq = lambda i: segs.seq_of_row(i * tm)
    in_specs = [pl.BlockSpec((tm, k), lambda i, j: (i, 0)),
                pl.BlockSpec((k, tn), lambda i, j: (0, j))]
    args = [x, w]
    if norm is not None:
        g, sh, sc = norm
        in_specs += [pl.BlockSpec((1, k), lambda i, j: (0, 0)),
                     pl.BlockSpec((None, 1, k), lambda i, j: (seq(i), 0, 0)),
                     pl.BlockSpec((None, 1, k), lambda i, j: (seq(i), 0, 0))]
        args += [g.reshape(1, k), sh, sc]
    if resid is not None:
        res, gate = resid
        in_specs += [pl.BlockSpec((tm, tn), lambda i, j: (i, j)),
                     pl.BlockSpec((None, 1, tn), lambda i, j: (seq(i), 0, j))]
        args += [res, gate]
    scratch = [pltpu.VMEM((tm, k), BF16)] if norm is not None else []
    return pl.pallas_call(
        functools.partial(_mm_kernel, norm=norm is not None, resid=resid is not None),
        grid=(m // tm, n // tn),
        in_specs=in_specs,
        out_specs=pl.BlockSpec((tm, tn), lambda i, j: (i, j)),
        out_shape=jax.ShapeDtypeStruct((m, n), out_dtype),
        scratch_shapes=scratch,
        compiler_params=_cparams(("parallel", "arbitrary")),
        name="dense_matmul",
    )(*args)


def _gmlp_kernel(u_ref, v_ref, lng_ref, lnb_ref, ws_ref, bs_ref, o_ref, *, chunks):
    for c in range(chunks):
        rows = slice(c * BLOCK, (c + 1) * BLOCK)
        for g in range(GM_GROUPS):
            cols = slice(g * GM_CH, (g + 1) * GM_CH)
            v = _gelu(v_ref[rows, cols])
            mu = jnp.mean(v, axis=-1, keepdims=True)
            vc = v - mu
            var = jnp.mean(vc * vc, axis=-1, keepdims=True)
            vn = vc * lax.rsqrt(var + EPS) * lng_ref[:, cols] + lnb_ref[:, cols]
            s = jnp.dot(ws_ref[g], vn.astype(BF16), preferred_element_type=F32) + bs_ref[g]
            o_ref[rows, cols] = (_gelu(u_ref[rows, cols]) * s).astype(o_ref.dtype)


def _gmlp(z, ln_g, ln_b, ws, bs):
    t = z.shape[0]
    rows = min(256, t)
    return pl.pallas_call(
        functools.partial(_gmlp_kernel, chunks=rows // BLOCK),
        grid=(t // rows,),
        in_specs=[pl.BlockSpec((rows, GM_WIDTH), lambda i: (i, 0)),
                  pl.BlockSpec((rows, GM_WIDTH), lambda i: (i, 1)),
                  pl.BlockSpec((1, GM_WIDTH), lambda i: (0, 0)),
                  pl.BlockSpec((1, GM_WIDTH), lambda i: (0, 0)),
                  pl.BlockSpec((GM_GROUPS, BLOCK, BLOCK), lambda i: (0, 0, 0)),
                  pl.BlockSpec((GM_GROUPS, BLOCK, 1), lambda i: (0, 0, 0))],
        out_specs=pl.BlockSpec((rows, GM_WIDTH), lambda i: (i, 0)),
        out_shape=jax.ShapeDtypeStruct((t, GM_WIDTH), BF16),
        compiler_params=_cparams(("parallel",)),
        name="gmlp_mixer",
    )(z, z, ln_g.reshape(1, GM_WIDTH), ln_b.reshape(1, GM_WIDTH), ws.astype(BF16),
      bs.reshape(GM_GROUPS, BLOCK, 1))


def _swa_kernel(sink_ref, q_ref, kp_ref, kc_ref, kn_ref, vp_ref, vc_ref, vn_ref, qg_ref, kg_ref, o_ref,
                *, segs):
    n = pl.program_id(0)
    pos, length = segs.pos_in_seq(n * BLOCK)
    has_prev = pos > 0
    has_next = pos + BLOCK < length

    qi = lax.broadcasted_iota(jnp.int32, (BLOCK, 3 * BLOCK), 0)
    sj = lax.broadcasted_iota(jnp.int32, (BLOCK, 3 * BLOCK), 1)
    rel = jnp.abs(qi + BLOCK - sj)
    valid = rel <= WINDOW
    valid = valid & ((sj >= BLOCK) | has_prev) & ((sj < 2 * BLOCK) | has_next)
    dist = rel.astype(F32)
    slopes = _alibi_slopes(SWA_HEADS)
    group = SWA_HEADS // SWA_KV
    scale = HEAD_DIM ** -0.5

    def rms(x, g):
        return x * lax.rsqrt(jnp.mean(x * x, axis=-1, keepdims=True) + EPS) * g

    kcat = jnp.concatenate([kp_ref[...], kc_ref[...], kn_ref[...]], axis=0)
    vcat = jnp.concatenate([vp_ref[...], vc_ref[...], vn_ref[...]], axis=0).astype(BF16)
    for kh in range(SWA_KV):
        cols = slice(kh * HEAD_DIM, (kh + 1) * HEAD_DIM)
        k = rms(kcat[:, cols], kg_ref[...]).astype(BF16)
        v = vcat[:, cols]
        for gi in range(group):
            h = kh * group + gi
            hc = slice(h * HEAD_DIM, (h + 1) * HEAD_DIM)
            q = (rms(q_ref[:, hc], qg_ref[...]) * scale).astype(BF16)
            s = lax.dot_general(q, k, (((1,), (1,)), ((), ())), preferred_element_type=F32)
            s = jnp.where(valid, s - slopes[h] * dist, NEG_BIG)
            sink = sink_ref[h]
            m = jnp.maximum(jnp.max(s, axis=-1, keepdims=True), sink)
            p = jnp.exp(s - m)
            denom = jnp.sum(p, axis=-1, keepdims=True) + jnp.exp(sink - m)
            o = jnp.dot(p.astype(BF16), v, preferred_element_type=F32)
            o_ref[:, hc] = (o / denom).astype(o_ref.dtype)


def _swa(segs, z, qn_g, kn_g, sink):
    t = z.shape[0]
    nb = t // BLOCK
    qw = SWA_HEADS * HEAD_DIM
    kw = SWA_KV * HEAD_DIM
    q_blk = 2 * GM_WIDTH // qw
    k_blk = (2 * GM_WIDTH + qw) // kw
    v_blk = k_blk + 1
    prev = lambda n: jnp.maximum(n - 1, 0)
    nxt = lambda n: jnp.minimum(n + 1, nb - 1)
    return pl.pallas_call(
        functools.partial(_swa_kernel, segs=segs),
        grid=(nb,),
        in_specs=[pl.BlockSpec(memory_space=pltpu.SMEM),
                  pl.BlockSpec((BLOCK, qw), lambda n: (n, q_blk)),
                  pl.BlockSpec((BLOCK, kw), lambda n: (prev(n), k_blk)),
                  pl.BlockSpec((BLOCK, kw), lambda n: (n, k_blk)),
                  pl.BlockSpec((BLOCK, kw), lambda n: (nxt(n), k_blk)),
                  pl.BlockSpec((BLOCK, kw), lambda n: (prev(n), v_blk)),
                  pl.BlockSpec((BLOCK, kw), lambda n: (n, v_blk)),
                  pl.BlockSpec((BLOCK, kw), lambda n: (nxt(n), v_blk)),
                  pl.BlockSpec((1, HEAD_DIM), lambda n: (0, 0)),
                  pl.BlockSpec((1, HEAD_DIM), lambda n: (0, 0))],
        out_specs=pl.BlockSpec((BLOCK, qw), lambda n: (n, 0)),
        out_shape=jax.ShapeDtypeStruct((t, qw), BF16),
        compiler_params=_cparams(("parallel",)),
        name="swa_mixer",
    )(sink.astype(F32), z, z, z, z, z, z, z, qn_g.reshape(1, HEAD_DIM), kn_g.reshape(1, HEAD_DIM))


QK_ONE, QK_HI, QK_LO = DIFF_DIM, DIFF_DIM + 3, DIFF_DIM + 6
KEY_BEFORE, KEY_AFTER, KEY_DIAG = 0, 1, 2
V_WIDTH = 2 * LANES


def _split3_np(x):
    x = np.float32(x)
    p1 = np.float32(x.astype(ml_dtypes.bfloat16))
    p2 = np.float32(np.float32(x - p1).astype(ml_dtypes.bfloat16))
    p3 = np.float32(np.float32(x - p1 - p2).astype(ml_dtypes.bfloat16))
    return p1, p2, p3


def _split3(x):
    p1 = x.astype(BF16).astype(F32)
    r1 = x - p1
    p2 = r1.astype(BF16).astype(F32)
    p3 = (r1 - p2).astype(BF16).astype(F32)
    return p1, p2, p3


def _key_extra_rows():
    rows = np.zeros((2, DIFF_HEADS, 1, LANES), np.float32)
    for h, slope in enumerate(_alibi_slopes(DIFF_HEADS)):
        sigma = LOG2E * slope
        for var, sign in ((KEY_BEFORE, -1.0), (KEY_AFTER, 1.0)):
            rows[var, h, 0, QK_HI:QK_HI + 3] = _split3_np(sign * 128.0 * sigma)
            rows[var, h, 0, QK_LO:QK_LO + 3] = _split3_np(sign * sigma)
    return rows


def _diff_prep_kernel(shift_ref, z_ref, qg_ref, kg_ref, seg_ref, kx_ref, q_ref, kt_ref, v_ref, *, segs, tm):
    width = DIFF_HEADS * 2 * DIFF_DIM
    hd = 2 * DIFF_DIM
    seg = seg_ref[...]
    lane = lax.broadcasted_iota(I32, (1, hd), 1)
    body = lane < DIFF_DIM
    c_shift = shift_ref[0]

    def seg_mean_sq(x):
        x2 = x * x
        hi = x2.astype(BF16)
        lo = (x2 - hi.astype(F32)).astype(BF16)
        tot = jnp.dot(hi, seg, preferred_element_type=F32) + jnp.dot(lo, seg, preferred_element_type=F32)
        return tot * (1.0 / DIFF_DIM)

    def put3(base, first_lane, pieces):
        out = base
        for n, p in enumerate(pieces):
            out = jnp.where(lane == first_lane + n, p, out)
        return out

    pos0, _ = segs.pos_in_seq(pl.program_id(0) * tm)
    pos = pos0 + lax.broadcasted_iota(I32, (tm, 1), 0)
    posf = pos.astype(F32)
    p_hi = (pos // LANES).astype(F32)
    p_lo = (pos % LANES).astype(F32)
    q_extra = put3(put3(put3(jnp.zeros((tm, hd), F32), QK_ONE, (1.0, 1.0, 1.0)), QK_HI, (p_hi, p_hi, p_hi)),
                   QK_LO, (p_lo, p_lo, p_lo))
    diag_extra = put3(jnp.zeros((1, hd), F32), QK_ONE, _split3(jnp.full((1, 1), -c_shift, F32)))
    ones_col = jnp.where(lane == 0, 1.0, 0.0).astype(BF16)

    for h, slope in enumerate(_alibi_slopes(DIFF_HEADS)):
        sigma = LOG2E * slope
        cols = slice(h * hd, (h + 1) * hd)
        q = z_ref[:, cols]
        qn = q * lax.rsqrt(seg_mean_sq(q) + EPS) * qg_ref[...] * (LOG2E * DIFF_DIM ** -0.5)
        k = z_ref[:, width + h * hd: width + (h + 1) * hd]
        kn = k * lax.rsqrt(seg_mean_sq(k) + EPS) * kg_ref[...]
        extras = {
            KEY_BEFORE: put3(jnp.broadcast_to(kx_ref[KEY_BEFORE, h], (tm, hd)), QK_ONE,
                             _split3(sigma * posf - c_shift)),
            KEY_AFTER: put3(jnp.broadcast_to(kx_ref[KEY_AFTER, h], (tm, hd)), QK_ONE,
                            _split3(-sigma * posf - c_shift)),
            KEY_DIAG: jnp.broadcast_to(diag_extra, (tm, hd)),
        }
        for c in range(2):
            qc = qn if c == 0 else pltpu.roll(qn, DIFF_DIM, 1)
            kc = kn if c == 0 else pltpu.roll(kn, DIFF_DIM, 1)
            q_ref[c, :, cols] = jnp.where(body, qc, q_extra).astype(BF16)
            for var in (KEY_BEFORE, KEY_AFTER, KEY_DIAG):
                kt_ref[var, c, h] = jnp.where(body, kc, extras[var]).T.astype(BF16)
        v_ref[:, h * V_WIDTH: h * V_WIDTH + hd] = z_ref[:, 2 * width + h * hd: 2 * width + (h + 1) * hd].astype(BF16)
        v_ref[:, h * V_WIDTH + hd: (h + 1) * V_WIDTH] = jnp.broadcast_to(ones_col, (tm, hd))


def _diff_prep(segs, z, qn_g, kn_g, c_shift):
    t = z.shape[0]
    width = DIFF_HEADS * 2 * DIFF_DIM
    hd = 2 * DIFF_DIM
    tm = min(256, t)
    seg = np.kron(np.eye(2, dtype=np.float32), np.ones((DIFF_DIM, DIFF_DIM), np.float32))
    qg2 = jnp.concatenate([qn_g, qn_g]).reshape(1, hd)
    kg2 = jnp.concatenate([kn_g, kn_g]).reshape(1, hd)
    return pl.pallas_call(
        functools.partial(_diff_prep_kernel, segs=segs, tm=tm),
        grid=(t // tm,),
        in_specs=[pl.BlockSpec(memory_space=pltpu.SMEM),
                  pl.BlockSpec((tm, 3 * width), lambda i: (i, 0)),
                  pl.BlockSpec((1, hd), lambda i: (0, 0)),
                  pl.BlockSpec((1, hd), lambda i: (0, 0)),
                  pl.BlockSpec((hd, hd), lambda i: (0, 0)),
                  pl.BlockSpec((2, DIFF_HEADS, 1, hd), lambda i: (0, 0, 0, 0))],
        out_specs=[pl.BlockSpec((2, tm, width), lambda i: (0, i, 0)),
                   pl.BlockSpec((3, 2, DIFF_HEADS, hd, tm), lambda i: (0, 0, 0, 0, i)),
                   pl.BlockSpec((tm, DIFF_HEADS * V_WIDTH), lambda i: (i, 0))],
        out_shape=[jax.ShapeDtypeStruct((2, t, width), BF16),
                   jax.ShapeDtypeStruct((3, 2, DIFF_HEADS, hd, t), BF16),
                   jax.ShapeDtypeStruct((t, DIFF_HEADS * V_WIDTH), BF16)],
        compiler_params=_cparams(("parallel",)),
        name="diff_prep",
    )(c_shift, z, qg2, kg2, jnp.asarray(seg, BF16), jnp.asarray(_key_extra_rows()))


def _diff_kernel(lam_ref, sigma_ref, q_ref, kt_ref, v_ref, sg_ref, o_ref, acc_sc, *m_sc,
                 tq, tk, out_scale, online):
    h = pl.program_id(1)
    i = pl.program_id(2)
    j = pl.program_id(3)
    hd = 2 * DIFF_DIM

    @pl.when(j == 0)
    def _():
        acc_sc[...] = jnp.zeros_like(acc_sc)
        if online:
            m_sc[0][...] = jnp.full_like(m_sc[0], NEG_BIG)

    def step(explicit_bias):
        if explicit_bias:
            qpos = i * tq + lax.broadcasted_iota(I32, (tq, tk), 0)
            kpos = j * tk + lax.broadcasted_iota(I32, (tq, tk), 1)
            bias = sigma_ref[h] * jnp.abs(qpos - kpos).astype(F32)
        v = v_ref[...]
        for c in range(2):
            s = jnp.dot(q_ref[c], kt_ref[c], preferred_element_type=F32)
            if explicit_bias:
                s = s - bias
            if online:
                m_prev = m_sc[0][c]
                m_new = jnp.maximum(m_prev, jnp.max(s, axis=-1, keepdims=True))
                p = jnp.exp2(s - m_new)
                acc_sc[c] = jnp.exp2(m_prev - m_new) * acc_sc[c] + jnp.dot(p.astype(BF16), v,
                                                                           preferred_element_type=F32)
                m_sc[0][c] = m_new
            else:
                acc_sc[c] += jnp.dot(jnp.exp2(s).astype(BF16), v, preferred_element_type=F32)

    if online:
        step(True)
    else:
        diag = (j * tk < (i + 1) * tq) & ((j + 1) * tk > i * tq)
        pl.when(diag)(lambda: step(True))
        pl.when(jnp.logical_not(diag))(lambda: step(False))

    @pl.when(j == pl.num_programs(3) - 1)
    def _():
        a0 = acc_sc[0]
        a1 = acc_sc[1]
        o = a0[:, :hd] / a0[:, hd:hd + 1] - lam_ref[0] * (a1[:, :hd] / a1[:, hd:hd + 1])
        o = o * lax.rsqrt(jnp.mean(o * o, axis=-1, keepdims=True) + EPS) * sg_ref[...] * out_scale
        o_ref[...] = o.astype(o_ref.dtype)


def _diff_attention(segs, q, kt, v, lam, subln_g, lam_init, online):
    hd = 2 * DIFF_DIM
    sigmas = jnp.asarray([LOG2E * s for s in _alibi_slopes(DIFF_HEADS)], F32)
    outs = []
    row0 = 0
    for n_seq, s_len in segs.groups:
        tq = min(1024, s_len)
        tk = min(512, s_len)
        qb0 = row0 // tq
        kb0 = row0 // tk
        nq = s_len // tq
        nk = s_len // tk

        def kt_map(b, h, i, j, kb0=kb0, nk=nk, tq=tq, tk=tk):
            if online:
                var = KEY_DIAG
            else:
                var = jnp.where((j + 1) * tk <= i * tq, KEY_BEFORE, jnp.where(j * tk >= (i + 1) * tq, KEY_AFTER,
                                                                              KEY_DIAG))
            return (var, 0, h, 0, kb0 + b * nk + j)

        scratch = [pltpu.VMEM((2, tq, V_WIDTH), F32)]
        if online:
            scratch.append(pltpu.VMEM((2, tq, 1), F32))
        out = pl.pallas_call(
            functools.partial(_diff_kernel, tq=tq, tk=tk, out_scale=1.0 - lam_init, online=online),
            grid=(n_seq, DIFF_HEADS, nq, nk),
            in_specs=[pl.BlockSpec(memory_space=pltpu.SMEM),
                      pl.BlockSpec(memory_space=pltpu.SMEM),
                      pl.BlockSpec((2, tq, hd), lambda b, h, i, j, qb0=qb0, nq=nq: (0, qb0 + b * nq + i, h)),
                      pl.BlockSpec((None, 2, None, hd, tk), kt_map),
                      pl.BlockSpec((tk, V_WIDTH), lambda b, h, i, j, kb0=kb0, nk=nk: (kb0 + b * nk + j, h)),
                      pl.BlockSpec((1, hd), lambda b, h, i, j: (0, 0))],
            out_specs=pl.BlockSpec((tq, hd), lambda b, h, i, j, nq=nq: (b * nq + i, h)),
            out_shape=jax.ShapeDtypeStruct((n_seq * s_len, DIFF_HEADS * hd), BF16),
            scratch_shapes=scratch,
            compiler_params=_cparams(("parallel", "parallel", "parallel", "arbitrary")),
            name="diff_attention_online" if online else "diff_attention",
        )(lam, sigmas, q, kt, v, subln_g.reshape(1, hd))
        outs.append(out)
        row0 += n_seq * s_len
    return jnp.concatenate(outs, axis=0)


def _diff_mixer(segs, z, qn_g, kn_g, lam, subln_g, lam_init):
    c_shift = (LOG2E * DIFF_DIM ** 0.5) * jnp.max(jnp.abs(qn_g)) * jnp.max(jnp.abs(kn_g))
    q, kt, v = _diff_prep(segs, z, qn_g, kn_g, c_shift.reshape(1).astype(F32))
    return lax.cond(c_shift < MAX_FIXED_SHIFT,
                    lambda: _diff_attention(segs, q, kt, v, lam, subln_g, lam_init, False),
                    lambda: _diff_attention(segs, q, kt, v, lam, subln_g, lam_init, True))


def _pack_bf16_pairs(x):
    n = x.shape[1] // 2
    lo = lax.bitcast_convert_type(x[:, :n].astype(BF16).astype(F32), U32) >> 16
    hi = lax.bitcast_convert_type(x[:, n:].astype(BF16).astype(F32), U32) & jnp.uint32(0xFFFF0000)
    return hi | lo


def _unpack_bf16_pairs(w):
    lo = lax.bitcast_convert_type(w << 16, F32)
    hi = lax.bitcast_convert_type(w & jnp.uint32(0xFFFF0000), F32)
    return lo, hi


def _router_kernel(x_ref, g_ref, sh_ref, sc_ref, wt_ref, rb_ref, tri_ref, ones_ref,
                   hp_ref, idx_ref, gate_ref, rank_ref, cnt_ref, run_sc):
    @pl.when(pl.program_id(0) == 0)
    def _():
        run_sc[...] = jnp.zeros_like(run_sc)

    h = _norm_modulate(x_ref[...], g_ref[...], sh_ref[...], sc_ref[...])
    hp_ref[...] = _pack_bf16_pairs(h)
    lt = lax.dot_general(wt_ref[...], h.astype(BF16), (((1,), (1,)), ((), ())), preferred_element_type=F32)
    tm = lt.shape[1]
    scores = jax.nn.sigmoid(lt)
    biased = scores + rb_ref[...]
    neg_inf = -jnp.inf

    gsz = N_EXPERTS // N_GROUPS
    lio = lax.broadcasted_iota(I32, (gsz, tm), 0)
    grp = []
    for g in range(N_GROUPS):
        blk = biased[g * gsz:(g + 1) * gsz]
        m1 = jnp.max(blk, axis=0, keepdims=True)
        i1 = jnp.min(jnp.where(blk == m1, lio, gsz), axis=0, keepdims=True)
        m2 = jnp.max(jnp.where(lio == i1, neg_inf, blk), axis=0, keepdims=True)
        grp.append(m1 + m2)
    chosen = [jnp.zeros((1, tm), jnp.bool_) for _ in range(N_GROUPS)]
    for _ in range(TOPK_GROUPS):
        work = [jnp.where(chosen[g], neg_inf, grp[g]) for g in range(N_GROUPS)]
        mx = functools.reduce(jnp.maximum, work)
        found = jnp.zeros((1, tm), jnp.bool_)
        for g in range(N_GROUPS):
            take = (work[g] == mx) & jnp.logical_not(found)
            chosen[g] = chosen[g] | take
            found = found | take
    masked = jnp.concatenate(
        [jnp.where(chosen[g], biased[g * gsz:(g + 1) * gsz], neg_inf) for g in range(N_GROUPS)], axis=0)

    eio = lax.broadcasted_iota(I32, (N_EXPERTS, tm), 0)
    kio = lax.broadcasted_iota(I32, (TOP_K, tm), 0)
    picks, gates = [], []
    for _ in range(TOP_K):
        mx = jnp.max(masked, axis=0, keepdims=True)
        ik = jnp.min(jnp.where(masked == mx, eio, N_EXPERTS), axis=0, keepdims=True)
        hit = eio == ik
        gates.append(jnp.sum(jnp.where(hit, scores, 0.0), axis=0, keepdims=True))
        masked = jnp.where(hit, neg_inf, masked)
        picks.append(ik)
    inv = ROUTED_SCALE / functools.reduce(jnp.add, gates)

    hit_all = jnp.zeros((N_EXPERTS, tm), F32)
    for ik in picks:
        hit_all = hit_all + jnp.where(eio == ik, 1.0, 0.0)
    hit_b = hit_all.astype(BF16)
    before = jnp.dot(hit_b, tri_ref[...], preferred_element_type=F32)
    before = before + jnp.concatenate([run_sc[...]] * (tm // LANES), axis=1)
    idx_out = jnp.zeros((TOP_K, tm), I32)
    gate_out = jnp.zeros((TOP_K, tm), F32)
    rank_out = jnp.zeros((TOP_K, tm), F32)
    for k, ik in enumerate(picks):
        rk = jnp.sum(jnp.where(eio == ik, before, 0.0), axis=0, keepdims=True)
        idx_out = jnp.where(kio == k, ik, idx_out)
        gate_out = jnp.where(kio == k, gates[k] * inv, gate_out)
        rank_out = jnp.where(kio == k, rk, rank_out)
    idx_ref[...] = idx_out
    gate_ref[...] = gate_out
    rank_ref[...] = rank_out.astype(I32)
    run_sc[...] += jnp.dot(hit_b, ones_ref[...], preferred_element_type=F32)
    cnt_ref[...] = run_sc[...]


def _router(segs, x, norm, router_w, router_b):
    t, d = x.shape
    tm = min(ROUTE_ROWS, t)
    g, sh, sc = norm
    seq = lambda i: segs.seq_of_row(i * tm)
    tri = np.triu(np.ones((tm, tm), np.float32), 1)
    tok = lambda dt: jax.ShapeDtypeStruct((TOP_K, t), dt)
    tok_spec = pl.BlockSpec((TOP_K, tm), lambda i: (0, i))
    return pl.pallas_call(
        _router_kernel,
        grid=(t // tm,),
        in_specs=[pl.BlockSpec((tm, d), lambda i: (i, 0)),
                  pl.BlockSpec((1, d), lambda i: (0, 0)),
                  pl.BlockSpec((None, 1, d), lambda i: (seq(i), 0, 0)),
                  pl.BlockSpec((None, 1, d), lambda i: (seq(i), 0, 0)),
                  pl.BlockSpec((N_EXPERTS, d), lambda i: (0, 0)),
                  pl.BlockSpec((N_EXPERTS, 1), lambda i: (0, 0)),
                  pl.BlockSpec((tm, tm), lambda i: (0, 0)),
                  pl.BlockSpec((tm, LANES), lambda i: (0, 0))],
        out_specs=[pl.BlockSpec((tm, d // 2), lambda i: (i, 0)), tok_spec, tok_spec, tok_spec,
                   pl.BlockSpec((N_EXPERTS, LANES), lambda i: (0, 0))],
        out_shape=[jax.ShapeDtypeStruct((t, d // 2), U32), tok(I32), tok(F32), tok(I32),
                   jax.ShapeDtypeStruct((N_EXPERTS, LANES), F32)],
        scratch_shapes=[pltpu.VMEM((N_EXPERTS, LANES), F32)],
        compiler_params=_cparams(("arbitrary",)),
        name="moe_router",
    )(x, g.reshape(1, d), sh, sc, router_w.T.astype(BF16), router_b.astype(F32).reshape(N_EXPERTS, 1),
      jnp.asarray(tri, BF16), jnp.ones((tm, LANES), BF16))


def _row_copy(src_ref, src_row, dst_ref, dst_row, sem):
    return pltpu.make_async_copy(src_ref.at[pl.ds(src_row, 1)], dst_ref.at[pl.ds(dst_row, 1)], sem)


def _dispatch_kernel(cend_ref, pend_ref, dest_ref, hp_ref, xs_ref, dest_sm, zrow, sems, *, tm):
    dest_cp = pltpu.make_async_copy(dest_ref.at[0], dest_sm, sems.at[0])
    dest_cp.start()

    @pl.when(pl.program_id(0) == 0)
    def _():
        zrow[...] = jnp.zeros_like(zrow)

        def per_expert(e, carry):
            def issue(r, c):
                _row_copy(zrow, 0, xs_ref, r, sems.at[1]).start()
                return c

            def drain(r, c):
                _row_copy(zrow, 0, xs_ref, r, sems.at[1]).wait()
                return c

            lax.fori_loop(cend_ref[e], pend_ref[e], issue, 0)
            lax.fori_loop(cend_ref[e], pend_ref[e], drain, 0)
            return carry

        lax.fori_loop(0, N_EXPERTS, per_expert, 0)

    dest_cp.wait()

    def issue(t, c):
        for k in range(TOP_K):
            _row_copy(hp_ref, t, xs_ref, dest_sm[0, k * tm + t], sems.at[2]).start()
        return c

    def drain(t, c):
        for k in range(TOP_K):
            _row_copy(hp_ref, t, xs_ref, dest_sm[0, k * tm + t], sems.at[2]).wait()
        return c

    lax.fori_loop(0, tm, issue, 0)
    lax.fori_loop(0, tm, drain, 0)


def _dispatch(hp, dest_tiles, cend, pend, n_rows):
    t, half = hp.shape
    tm = dest_tiles.shape[-1] // TOP_K
    grid_spec = pltpu.PrefetchScalarGridSpec(
        num_scalar_prefetch=2,
        grid=(t // tm,),
        in_specs=[pl.BlockSpec((1, 1, TOP_K * tm), lambda i, ce, pe: (i, 0, 0)),
                  pl.BlockSpec((tm, half), lambda i, ce, pe: (i, 0))],
        out_specs=pl.BlockSpec(memory_space=pl.ANY),
        scratch_shapes=[pltpu.SMEM((1, TOP_K * tm), I32), pltpu.VMEM((8, half), U32),
                        pltpu.SemaphoreType.DMA((3,))],
    )
    return pl.pallas_call(
        functools.partial(_dispatch_kernel, tm=tm),
        grid_spec=grid_spec,
        out_shape=jax.ShapeDtypeStruct((n_rows, half), U32),
        compiler_params=_cparams(("arbitrary",)),
        name="moe_dispatch",
    )(cend, pend, dest_tiles, hp)


def _expert_kernel(be_ref, first_ref, used_ref, x_ref, w1_ref, w3_ref, w2_ref, o_ref, w1_sc, w3_sc, w2_sc):
    i = pl.program_id(0)

    @pl.when(first_ref[i] == 1)
    def _():
        w1_sc[...] = w1_ref[...].astype(BF16)
        w3_sc[...] = w3_ref[...].astype(BF16)
        w2_sc[...] = w2_ref[...].astype(BF16)

    @pl.when(i < used_ref[0])
    def _():
        lo, hi = _unpack_bf16_pairs(x_ref[...])
        lo = lo.astype(BF16)
        hi = hi.astype(BF16)
        half = lo.shape[1]
        a = (jnp.dot(lo, w1_sc[:half], preferred_element_type=F32)
             + jnp.dot(hi, w1_sc[half:], preferred_element_type=F32))
        b = (jnp.dot(lo, w3_sc[:half], preferred_element_type=F32)
             + jnp.dot(hi, w3_sc[half:], preferred_element_type=F32))
        hb = (_silu(a) * b).astype(BF16)
        o_ref[...] = _pack_bf16_pairs(jnp.dot(hb, w2_sc[...], preferred_element_type=F32))

    @pl.when(i >= used_ref[0])
    def _():
        o_ref[...] = jnp.zeros_like(o_ref)


def _experts(xs, block_e, first, used, w1, w3, w2, layer):
    rows, half = xs.shape
    d = 2 * half
    tm = MOE_ROWS
    n_blocks = rows // tm
    de = w1.shape[-1]
    grid_spec = pltpu.PrefetchScalarGridSpec(
        num_scalar_prefetch=3,
        grid=(n_blocks,),
        in_specs=[pl.BlockSpec((tm, half), lambda i, be, fi, us: (i, 0)),
                  pl.BlockSpec((None, None, d, de), lambda i, be, fi, us: (layer, be[i], 0, 0)),
                  pl.BlockSpec((None, None, d, de), lambda i, be, fi, us: (layer, be[i], 0, 0)),
                  pl.BlockSpec((None, None, de, d), lambda i, be, fi, us: (layer, be[i], 0, 0))],
        out_specs=pl.BlockSpec((tm, half), lambda i, be, fi, us: (i, 0)),
        scratch_shapes=[pltpu.VMEM((d, de), BF16), pltpu.VMEM((d, de), BF16), pltpu.VMEM((de, d), BF16)],
    )
    return pl.pallas_call(
        _expert_kernel,
        grid_spec=grid_spec,
        out_shape=jax.ShapeDtypeStruct((rows, half), U32),
        compiler_params=_cparams(("arbitrary",)),
        name="swiglu_experts",
    )(block_e, first, used, xs, w1, w3, w2)


def _block_plan(counts, tm, n_blocks):
    padded = (counts + tm - 1) // tm * tm
    pad_end = jnp.cumsum(padded)
    pad_start = pad_end - padded
    block_e = jnp.minimum(jnp.searchsorted(pad_end, jnp.arange(n_blocks, dtype=I32) * tm, side='right'),
                          N_EXPERTS - 1).astype(I32)
    first = jnp.concatenate([jnp.ones((1,), I32), (block_e[1:] != block_e[:-1]).astype(I32)])
    used = (pad_end[-1:] // tm).astype(I32)
    return pad_start.astype(I32), pad_end.astype(I32), block_e, first, used


def _combine_kernel(dest_ref, x_ref, gate_ref, ysh_ref, g_ref, ys_ref, o_ref, dest_sm, buf, sems, *, tm):
    dest_cp = pltpu.make_async_copy(dest_ref.at[0], dest_sm, sems.at[0])
    dest_cp.start()
    dest_cp.wait()

    def issue(t, c):
        for k in range(TOP_K):
            _row_copy(ys_ref, dest_sm[0, k * tm + t], buf.at[k], t, sems.at[1]).start()
        return c

    def drain(t, c):
        for k in range(TOP_K):
            _row_copy(ys_ref, dest_sm[0, k * tm + t], buf.at[k], t, sems.at[1]).wait()
        return c

    lax.fori_loop(0, tm, issue, 0)
    lax.fori_loop(0, tm, drain, 0)

    lo, hi = _unpack_bf16_pairs(ysh_ref[...])
    for k in range(TOP_K):
        gk = g_ref[:, k:k + 1]
        rl, rh = _unpack_bf16_pairs(buf[k])
        lo = lo + gk * rl
        hi = hi + gk * rh
    half = lo.shape[1]
    o_ref[:, :half] = x_ref[:, :half] + gate_ref[:, :half] * lo
    o_ref[:, half:] = x_ref[:, half:] + gate_ref[:, half:] * hi


def _combine(segs, x, gate, ysh, ys, g_tok, dest_tiles):
    t, d = x.shape
    half = d // 2
    tm = dest_tiles.shape[-1] // TOP_K
    seq = lambda i: segs.seq_of_row(i * tm)
    return pl.pallas_call(
        functools.partial(_combine_kernel, tm=tm),
        grid=(t // tm,),
        in_specs=[pl.BlockSpec((1, 1, TOP_K * tm), lambda i: (i, 0, 0)),
                  pl.BlockSpec((tm, d), lambda i: (i, 0)),
                  pl.BlockSpec((None, 1, d), lambda i: (seq(i), 0, 0)),
                  pl.BlockSpec((tm, half), lambda i: (i, 0)),
                  pl.BlockSpec((tm, TOP_K), lambda i: (i, 0)),
                  pl.BlockSpec(memory_space=pl.ANY)],
        out_specs=pl.BlockSpec((tm, d), lambda i: (i, 0)),
        out_shape=jax.ShapeDtypeStruct((t, d), F32),
        scratch_shapes=[pltpu.SMEM((1, TOP_K * tm), I32), pltpu.VMEM((TOP_K, tm, half), U32),
                        pltpu.SemaphoreType.DMA((2,))],
        compiler_params=_cparams(("arbitrary",)),
        name="moe_combine",
    )(dest_tiles, x, gate, ysh, g_tok, ys)


def _moe_layer(segs, x, norm, gate, router_w, router_b, exp_w1, exp_w3, exp_w2, sh_w1, sh_w3, sh_w2, layer):
    t, d = x.shape
    hp, idx, g, rank, cnt = _router(segs, x, norm, router_w, router_b)
    counts = cnt[:, 0].astype(I32)
    n_blocks = (t * TOP_K) // MOE_ROWS + N_EXPERTS
    pad_start, pad_end, block_e, first, used = _block_plan(counts, MOE_ROWS, n_blocks)
    dest = pad_start[idx] + rank
    tm = min(ROUTE_ROWS, t)
    dest_tiles = dest.reshape(TOP_K, t // tm, tm).transpose(1, 0, 2).reshape(t // tm, 1, TOP_K * tm)
    xs = _dispatch(hp, dest_tiles, pad_start + counts, pad_end, n_blocks * MOE_ROWS)
    ys = _experts(xs, block_e, first, used, exp_w1, exp_w3, exp_w2, layer)
    n_sh = t // MOE_ROWS
    zeros = jnp.zeros((n_sh,), I32)
    ysh = _experts(hp, zeros + layer, zeros.at[0].set(1), jnp.full((1,), n_sh, I32),
                   sh_w1[None], sh_w3[None], sh_w2[None], 0)
    return _combine(segs, x, gate, ysh, ys, g.T, dest_tiles)


def kernel(x_prompt, x_sample, c_prompt, c_sample, w_ada, b_ada, norm1_g, norm2_g, w_in_e, gm_ln_g, gm_ln_b,
           gm_ws, gm_bs, swa_qn_g, swa_kn_g, swa_sink, w_out_e, w_in_o, diff_qn_g, diff_kn_g, lam_q1, lam_k1,
           lam_q2, lam_k2, diff_subln_g, w_out_o, router_w, router_b, exp_w1, exp_w3, exp_w2, sh_w1, sh_w3,
           sh_w2):
    d = x_prompt.shape[-1]
    depth = w_ada.shape[0]
    segs = _Segs([x_prompt.shape[:2], x_sample.shape[:2]])
    n_seq = segs.n_seq
    x = jnp.concatenate([x_prompt.reshape(-1, d), x_sample.reshape(-1, d)], axis=0)

    c_rows = -(-n_seq // 16) * 16
    c_all = jnp.zeros((c_rows, d), F32).at[:n_seq].set(jnp.concatenate([c_prompt, c_sample], axis=0))
    mod = _ada(c_all, w_ada, b_ada)[:, :n_seq]

    for l in range(depth):
        sh1, sc1, g1, sh2, sc2, g2 = [m.reshape(n_seq, 1, d) for m in jnp.split(mod[l], 6, axis=-1)]
        j = l // 2
        if l % 2 == 0:
            z = _matmul(segs, x, w_in_e[j].astype(BF16), norm=(norm1_g[l], sh1, sc1))
            a = _gmlp(z, gm_ln_g[j], gm_ln_b[j], gm_ws[j], gm_bs[j])
            b = _swa(segs, z, swa_qn_g[j], swa_kn_g[j], swa_sink[j])
            mix = jnp.concatenate([a, b], axis=-1)
            x = _matmul(segs, mix, w_out_e[j].astype(BF16), resid=(x, g1))
        else:
            lam_init = 0.8 - 0.6 * math.exp(-0.3 * l)
            z = _matmul(segs, x, w_in_o[j].astype(BF16), norm=(norm1_g[l], sh1, sc1))
            lam = (jnp.exp(jnp.sum(lam_q1[j].astype(F32) * lam_k1[j].astype(F32)))
                   - jnp.exp(jnp.sum(lam_q2[j].astype(F32) * lam_k2[j].astype(F32))) + lam_init).reshape(1)
            o = _diff_mixer(segs, z, diff_qn_g[j], diff_kn_g[j], lam, diff_subln_g[j], lam_init)
            x = _matmul(segs, o, w_out_o[j].astype(BF16), resid=(x, g1))
        x = _moe_layer(segs, x, (norm2_g[l], sh2, sc2), g2, router_w[l], router_b[l], exp_w1, exp_w3, exp_w2,
                       sh_w1, sh_w3, sh_w2, l)

    t0 = x_prompt.shape[0] * x_prompt.shape[1]
    return x[:t0].reshape(x_prompt.shape), x[t0:].reshape(x_sample.shape)
```

```python
import functools
import math

import ml_dtypes
import numpy as np
import jax
import jax.numpy as jnp
from jax import lax
from jax.experimental import pallas as pl
from jax.experimental.pallas import tpu as pltpu

F32 = jnp.float32
BF16 = jnp.bfloat16
U32 = jnp.uint32
I32 = jnp.int32

BLOCK = 128
GM_GROUPS = 8
GM_CH = 128
GM_WIDTH = GM_GROUPS * GM_CH
SWA_HEADS = 8
SWA_KV = 2
HEAD_DIM = 128
WINDOW = 128
DIFF_HEADS = 16
DIFF_DIM = 64
N_EXPERTS = 256
TOP_K = 8
N_GROUPS = 8
TOPK_GROUPS = 4
D_EXPERT = 512
ROUTED_SCALE = 2.5
EPS = 1e-6
NEG_BIG = -1e30
LOG2E = 1.4426950408889634

LANES = 128
VMEM_LIMIT = 56 * 1024 * 1024
MOE_ROWS = 256
ROUTE_ROWS = 256
MAX_FIXED_SHIFT = 50.0


def _cparams(sem):
    return pltpu.CompilerParams(dimension_semantics=sem, vmem_limit_bytes=VMEM_LIMIT)


def _alibi_slopes(n):
    return [2.0 ** (-8.0 * (h + 1) / n) for h in range(n)]


def _gelu(x):
    return 0.5 * x * (1.0 + jnp.tanh(0.7978845608028654 * (x + 0.044715 * x * x * x)))


def _silu(x):
    return x * jax.nn.sigmoid(x)


class _Segs:
    def __init__(self, groups):
        self.groups = tuple(groups)
        self.tokens = sum(b * s for b, s in groups)
        self.n_seq = sum(b for b, _ in groups)

    def seq_of_row(self, r):
        out = None
        row0, seq0 = 0, 0
        for b, s in self.groups:
            val = seq0 + (r - row0) // s
            out = val if out is None else jnp.where(r >= row0, val, out)
            row0 += b * s
            seq0 += b
        return out

    def pos_in_seq(self, r):
        pos, length = None, None
        row0 = 0
        for b, s in self.groups:
            p = (r - row0) % s
            pos = p if pos is None else jnp.where(r >= row0, p, pos)
            length = s if length is None else jnp.where(r >= row0, s, length)
            row0 += b * s
        return pos, length


def _ada_kernel(c_ref, w_ref, b_ref, o_ref):
    a = _silu(c_ref[...]).astype(BF16)
    o_ref[...] = jnp.dot(a, w_ref[...].astype(BF16), preferred_element_type=F32) + b_ref[...]


def _ada(c_all, w_ada, b_ada):
    depth, d, n = w_ada.shape
    rows = c_all.shape[0]
    tn = 1024
    return pl.pallas_call(
        _ada_kernel,
        grid=(depth, n // tn),
        in_specs=[pl.BlockSpec((rows, d), lambda l, j: (0, 0)),
                  pl.BlockSpec((None, d, tn), lambda l, j: (l, 0, j)),
                  pl.BlockSpec((None, 1, tn), lambda l, j: (l, 0, j))],
        out_specs=pl.BlockSpec((None, rows, tn), lambda l, j: (l, 0, j)),
        out_shape=jax.ShapeDtypeStruct((depth, rows, n), F32),
        compiler_params=_cparams(("parallel", "parallel")),
        name="ada_modulation",
    )(c_all, w_ada, b_ada.reshape(depth, 1, n))


def _norm_modulate(x, g, shift, scale):
    y = x * lax.rsqrt(jnp.mean(x * x, axis=-1, keepdims=True) + EPS)
    return (y * g) * (1.0 + scale) + shift


def _mm_kernel(*refs, norm, resid):
    it = iter(refs)
    x_ref = next(it)
    w_ref = next(it)
    if norm:
        g_ref, sh_ref, sc_ref = next(it), next(it), next(it)
    if resid:
        res_ref, gate_ref = next(it), next(it)
    o_ref = next(it)
    h_sc = next(it) if norm else None

    if norm:
        @pl.when(pl.program_id(1) == 0)
        def _():
            h_sc[...] = _norm_modulate(x_ref[...], g_ref[...], sh_ref[...], sc_ref[...]).astype(BF16)
        lhs = h_sc[...]
    else:
        lhs = x_ref[...]
    acc = jnp.dot(lhs, w_ref[...], preferred_element_type=F32)
    if resid:
        acc = res_ref[...] + gate_ref[...] * acc
    o_ref[...] = acc.astype(o_ref.dtype)


def _matmul(segs, x, w, *, norm=None, resid=None, out_dtype=F32, tm=1024):
    m, k = x.shape
    n = w.shape[1]
    tm = min([tm] + [s for _, s in segs.groups])
    tn = next(c for c in (1024, 512, 256, n) if n % c == 0)
    seq = lambda i: segs.seq_of_row(i * tm)
    in_specs = [pl.BlockSpec((tm, k), lambda i, j: (i, 0)),
                pl.BlockSpec((k, tn), lambda i, j: (0, j))]
    args = [x, w]
    if norm is not None:
        g, sh, sc = norm
        in_specs += [pl.BlockSpec((1, k), lambda i, j: (0, 0)),
                     pl.BlockSpec((None, 1, k), lambda i, j: (seq(i), 0, 0)),
                     pl.BlockSpec((None, 1, k), lambda i, j: (seq(i), 0, 0))]
        args += [g.reshape(1, k), sh, sc]
    if resid is not None:
        res, gate = resid
        in_specs += [pl.BlockSpec((tm, tn), lambda i, j: (i, j)),
                     pl.BlockSpec((None, 1, tn), lambda i, j: (seq(i), 0, j))]
        args += [res, gate]
    scratch = [pltpu.VMEM((tm, k), BF16)] if norm is not None else []
    return pl.pallas_call(
        functools.partial(_mm_kernel, norm=norm is not None, resid=resid is not None),
        grid=(m // tm, n // tn),
        in_specs=in_specs,
        out_specs=pl.BlockSpec((tm, tn), lambda i, j: (i, j)),
        out_shape=jax.ShapeDtypeStruct((m, n), out_dtype),
        scratch_shapes=scratch,
        compiler_params=_cparams(("parallel", "arbitrary")),
        name="dense_matmul",
    )(*args)


def _gmlp_kernel(u_ref, v_ref, lng_ref, lnb_ref, ws_ref, bs_ref, o_ref, *, chunks):
    for c in range(chunks):
        rows = slice(c * BLOCK, (c + 1) * BLOCK)
        for g in range(GM_GROUPS):
            cols = slice(g * GM_CH, (g + 1) * GM_CH)
            v = _gelu(v_ref[rows, cols])
            mu = jnp.mean(v, axis=-1, keepdims=True)
            vc = v - mu
            var = jnp.mean(vc * vc, axis=-1, keepdims=True)
            vn = vc * lax.rsqrt(var + EPS) * lng_ref[:, cols] + lnb_ref[:, cols]
            s = jnp.dot(ws_ref[g], vn.astype(BF16), preferred_element_type=F32) + bs_ref[g]
            o_ref[rows, cols] = (_gelu(u_ref[rows, cols]) * s).astype(o_ref.dtype)


def _gmlp(z, ln_g, ln_b, ws, bs):
    t = z.shape[0]
    rows = min(256, t)
    return pl.pallas_call(
        functools.partial(_gmlp_kernel, chunks=rows // BLOCK),
        grid=(t // rows,),
        in_specs=[pl.BlockSpec((rows, GM_WIDTH), lambda i: (i, 0)),
                  pl.BlockSpec((rows, GM_WIDTH), lambda i: (i, 1)),
                  pl.BlockSpec((1, GM_WIDTH), lambda i: (0, 0)),
                  pl.BlockSpec((1, GM_WIDTH), lambda i: (0, 0)),
                  pl.BlockSpec((GM_GROUPS, BLOCK, BLOCK), lambda i: (0, 0, 0)),
                  pl.BlockSpec((GM_GROUPS, BLOCK, 1), lambda i: (0, 0, 0))],
        out_specs=pl.BlockSpec((rows, GM_WIDTH), lambda i: (i, 0)),
        out_shape=jax.ShapeDtypeStruct((t, GM_WIDTH), BF16),
        compiler_params=_cparams(("parallel",)),
        name="gmlp_mixer",
    )(z, z, ln_g.reshape(1, GM_WIDTH), ln_b.reshape(1, GM_WIDTH), ws.astype(BF16),
      bs.reshape(GM_GROUPS, BLOCK, 1))


def _swa_kernel(sink_ref, q_ref, kp_ref, kc_ref, kn_ref, vp_ref, vc_ref, vn_ref, qg_ref, kg_ref, o_ref,
                *, segs):
    n = pl.program_id(0)
    pos, length = segs.pos_in_seq(n * BLOCK)
    has_prev = pos > 0
    has_next = pos + BLOCK < length

    qi = lax.broadcasted_iota(jnp.int32, (BLOCK, 3 * BLOCK), 0)
    sj = lax.broadcasted_iota(jnp.int32, (BLOCK, 3 * BLOCK), 1)
    rel = jnp.abs(qi + BLOCK - sj)
    valid = rel <= WINDOW
    valid = valid & ((sj >= BLOCK) | has_prev) & ((sj < 2 * BLOCK) | has_next)
    dist = rel.astype(F32)
    slopes = _alibi_slopes(SWA_HEADS)
    group = SWA_HEADS // SWA_KV
    scale = HEAD_DIM ** -0.5

    def rms(x, g):
        return x * lax.rsqrt(jnp.mean(x * x, axis=-1, keepdims=True) + EPS) * g

    kcat = jnp.concatenate([kp_ref[...], kc_ref[...], kn_ref[...]], axis=0)
    vcat = jnp.concatenate([vp_ref[...], vc_ref[...], vn_ref[...]], axis=0).astype(BF16)
    for kh in range(SWA_KV):
        cols = slice(kh * HEAD_DIM, (kh + 1) * HEAD_DIM)
        k = rms(kcat[:, cols], kg_ref[...]).astype(BF16)
        v = vcat[:, cols]
        for gi in range(group):
            h = kh * group + gi
            hc = slice(h * HEAD_DIM, (h + 1) * HEAD_DIM)
            q = (rms(q_ref[:, hc], qg_ref[...]) * scale).astype(BF16)
            s = lax.dot_general(q, k, (((1,), (1,)), ((), ())), preferred_element_type=F32)
            s = jnp.where(valid, s - slopes[h] * dist, NEG_BIG)
            sink = sink_ref[h]
            m = jnp.maximum(jnp.max(s, axis=-1, keepdims=True), sink)
            p = jnp.exp(s - m)
            denom = jnp.sum(p, axis=-1, keepdims=True) + jnp.exp(sink - m)
            o = jnp.dot(p.astype(BF16), v, preferred_element_type=F32)
            o_ref[:, hc] = (o / denom).astype(o_ref.dtype)


def _swa(segs, z, qn_g, kn_g, sink):
    t = z.shape[0]
    nb = t // BLOCK
    qw = SWA_HEADS * HEAD_DIM
    kw = SWA_KV * HEAD_DIM
    q_blk = 2 * GM_WIDTH // qw
    k_blk = (2 * GM_WIDTH + qw) // kw
    v_blk = k_blk + 1
    prev = lambda n: jnp.maximum(n - 1, 0)
    nxt = lambda n: jnp.minimum(n + 1, nb - 1)
    return pl.pallas_call(
        functools.partial(_swa_kernel, segs=segs),
        grid=(nb,),
        in_specs=[pl.BlockSpec(memory_space=pltpu.SMEM),
                  pl.BlockSpec((BLOCK, qw), lambda n: (n, q_blk)),
                  pl.BlockSpec((BLOCK, kw), lambda n: (prev(n), k_blk)),
                  pl.BlockSpec((BLOCK, kw), lambda n: (n, k_blk)),
                  pl.BlockSpec((BLOCK, kw), lambda n: (nxt(n), k_blk)),
                  pl.BlockSpec((BLOCK, kw), lambda n: (prev(n), v_blk)),
                  pl.BlockSpec((BLOCK, kw), lambda n: (n, v_blk)),
                  pl.BlockSpec((BLOCK, kw), lambda n: (nxt(n), v_blk)),
                  pl.BlockSpec((1, HEAD_DIM), lambda n: (0, 0)),
                  pl.BlockSpec((1, HEAD_DIM), lambda n: (0, 0))],
        out_specs=pl.BlockSpec((BLOCK, qw), lambda n: (n, 0)),
        out_shape=jax.ShapeDtypeStruct((t, qw), BF16),
        compiler_params=_cparams(("parallel",)),
        name="swa_mixer",
    )(sink.astype(F32), z, z, z, z, z, z, z, qn_g.reshape(1, HEAD_DIM), kn_g.reshape(1, HEAD_DIM))


QK_ONE, QK_HI, QK_LO = DIFF_DIM, DIFF_DIM + 3, DIFF_DIM + 6
KEY_BEFORE, KEY_AFTER, KEY_DIAG = 0, 1, 2
V_WIDTH = 2 * LANES


def _split3_np(x):
    x = np.float32(x)
    p1 = np.float32(x.astype(ml_dtypes.bfloat16))
    p2 = np.float32(np.float32(x - p1).astype(ml_dtypes.bfloat16))
    p3 = np.float32(np.float32(x - p1 - p2).astype(ml_dtypes.bfloat16))
    return p1, p2, p3


def _split3(x):
    p1 = x.astype(BF16).astype(F32)
    r1 = x - p1
    p2 = r1.astype(BF16).astype(F32)
    p3 = (r1 - p2).astype(BF16).astype(F32)
    return p1, p2, p3


def _key_extra_rows():
    rows = np.zeros((2, DIFF_HEADS, 1, LANES), np.float32)
    for h, slope in enumerate(_alibi_slopes(DIFF_HEADS)):
        sigma = LOG2E * slope
        for var, sign in ((KEY_BEFORE, -1.0), (KEY_AFTER, 1.0)):
            rows[var, h, 0, QK_HI:QK_HI + 3] = _split3_np(sign * 128.0 * sigma)
            rows[var, h, 0, QK_LO:QK_LO + 3] = _split3_np(sign * sigma)
    return rows


def _diff_prep_kernel(shift_ref, z_ref, qg_ref, kg_ref, seg_ref, kx_ref, q_ref, kt_ref, v_ref, *, segs, tm):
    width = DIFF_HEADS * 2 * DIFF_DIM
    hd = 2 * DIFF_DIM
    seg = seg_ref[...]
    lane = lax.broadcasted_iota(I32, (1, hd), 1)
    body = lane < DIFF_DIM
    c_shift = shift_ref[0]

    def seg_mean_sq(x):
        x2 = x * x
        hi = x2.astype(BF16)
        lo = (x2 - hi.astype(F32)).astype(BF16)
        tot = jnp.dot(hi, seg, preferred_element_type=F32) + jnp.dot(lo, seg, preferred_element_type=F32)
        return tot * (1.0 / DIFF_DIM)

    def put3(base, first_lane, pieces):
        out = base
        for n, p in enumerate(pieces):
            out = jnp.where(lane == first_lane + n, p, out)
        return out

    pos0, _ = segs.pos_in_seq(pl.program_id(0) * tm)
    pos = pos0 + lax.broadcasted_iota(I32, (tm, 1), 0)
    posf = pos.astype(F32)
    p_hi = (pos // LANES).astype(F32)
    p_lo = (pos % LANES).astype(F32)
    q_extra = put3(put3(put3(jnp.zeros((tm, hd), F32), QK_ONE, (1.0, 1.0, 1.0)), QK_HI, (p_hi, p_hi, p_hi)),
                   QK_LO, (p_lo, p_lo, p_lo))
    diag_extra = put3(jnp.zeros((1, hd), F32), QK_ONE, _split3(jnp.full((1, 1), -c_shift, F32)))
    ones_col = jnp.where(lane == 0, 1.0, 0.0).astype(BF16)

    for h, slope in enumerate(_alibi_slopes(DIFF_HEADS)):
        sigma = LOG2E * slope
        cols = slice(h * hd, (h + 1) * hd)
        q = z_ref[:, cols]
        qn = q * lax.rsqrt(seg_mean_sq(q) + EPS) * qg_ref[...] * (LOG2E * DIFF_DIM ** -0.5)
        k = z_ref[:, width + h * hd: width + (h + 1) * hd]
        kn = k * lax.rsqrt(seg_mean_sq(k) + EPS) * kg_ref[...]
        extras = {
            KEY_BEFORE: put3(jnp.broadcast_to(kx_ref[KEY_BEFORE, h], (tm, hd)), QK_ONE,
                             _split3(sigma * posf - c_shift)),
            KEY_AFTER: put3(jnp.broadcast_to(kx_ref[KEY_AFTER, h], (tm, hd)), QK_ONE,
                            _split3(-sigma * posf - c_shift)),
            KEY_DIAG: jnp.broadcast_to(diag_extra, (tm, hd)),
        }
        for c in range(2):
            qc = qn if c == 0 else pltpu.roll(qn, DIFF_DIM, 1)
            kc = kn if c == 0 else pltpu.roll(kn, DIFF_DIM, 1)
            q_ref[c, :, cols] = jnp.where(body, qc, q_extra).astype(BF16)
            for var in (KEY_BEFORE, KEY_AFTER, KEY_DIAG):
                kt_ref[var, c, h] = jnp.where(body, kc, extras[var]).T.astype(BF16)
        v_ref[:, h * V_WIDTH: h * V_WIDTH + hd] = z_ref[:, 2 * width + h * hd: 2 * width + (h + 1) * hd].astype(BF16)
        v_ref[:, h * V_WIDTH + hd: (h + 1) * V_WIDTH] = jnp.broadcast_to(ones_col, (tm, hd))


def _diff_prep(segs, z, qn_g, kn_g, c_shift):
    t = z.shape[0]
    width = DIFF_HEADS * 2 * DIFF_DIM
    hd = 2 * DIFF_DIM
    tm = min(256, t)
    seg = np.kron(np.eye(2, dtype=np.float32), np.ones((DIFF_DIM, DIFF_DIM), np.float32))
    qg2 = jnp.concatenate([qn_g, qn_g]).reshape(1, hd)
    kg2 = jnp.concatenate([kn_g, kn_g]).reshape(1, hd)
    return pl.pallas_call(
        functools.partial(_diff_prep_kernel, segs=segs, tm=tm),
        grid=(t // tm,),
        in_specs=[pl.BlockSpec(memory_space=pltpu.SMEM),
                  pl.BlockSpec((tm, 3 * width), lambda i: (i, 0)),
                  pl.BlockSpec((1, hd), lambda i: (0, 0)),
                  pl.BlockSpec((1, hd), lambda i: (0, 0)),
                  pl.BlockSpec((hd, hd), lambda i: (0, 0)),
                  pl.BlockSpec((2, DIFF_HEADS, 1, hd), lambda i: (0, 0, 0, 0))],
        out_specs=[pl.BlockSpec((2, tm, width), lambda i: (0, i, 0)),
                   pl.BlockSpec((3, 2, DIFF_HEADS, hd, tm), lambda i: (0, 0, 0, 0, i)),
                   pl.BlockSpec((tm, DIFF_HEADS * V_WIDTH), lambda i: (i, 0))],
        out_shape=[jax.ShapeDtypeStruct((2, t, width), BF16),
                   jax.ShapeDtypeStruct((3, 2, DIFF_HEADS, hd, t), BF16),
                   jax.ShapeDtypeStruct((t, DIFF_HEADS * V_WIDTH), BF16)],
        compiler_params=_cparams(("parallel",)),
        name="diff_prep",
    )(c_shift, z, qg2, kg2, jnp.asarray(seg, BF16), jnp.asarray(_key_extra_rows()))


def _diff_kernel(lam_ref, sigma_ref, q_ref, kt_ref, v_ref, sg_ref, o_ref, acc_sc, *m_sc,
                 tq, tk, out_scale, online):
    h = pl.program_id(1)
    i = pl.program_id(2)
    j = pl.program_id(3)
    hd = 2 * DIFF_DIM

    @pl.when(j == 0)
    def _():
        acc_sc[...] = jnp.zeros_like(acc_sc)
        if online:
            m_sc[0][...] = jnp.full_like(m_sc[0], NEG_BIG)

    def step(explicit_bias):
        if explicit_bias:
            qpos = i * tq + lax.broadcasted_iota(I32, (tq, tk), 0)
            kpos = j * tk + lax.broadcasted_iota(I32, (tq, tk), 1)
            bias = sigma_ref[h] * jnp.abs(qpos - kpos).astype(F32)
        v = v_ref[...]
        for c in range(2):
            s = jnp.dot(q_ref[c], kt_ref[c], preferred_element_type=F32)
            if explicit_bias:
                s = s - bias
            if online:
                m_prev = m_sc[0][c]
                m_new = jnp.maximum(m_prev, jnp.max(s, axis=-1, keepdims=True))
                p = jnp.exp2(s - m_new)
                acc_sc[c] = jnp.exp2(m_prev - m_new) * acc_sc[c] + jnp.dot(p.astype(BF16), v,
                                                                           preferred_element_type=F32)
                m_sc[0][c] = m_new
            else:
                acc_sc[c] += jnp.dot(jnp.exp2(s).astype(BF16), v, preferred_element_type=F32)

    if online:
        step(True)
    else:
        diag = (j * tk < (i + 1) * tq) & ((j + 1) * tk > i * tq)
        pl.when(diag)(lambda: step(True))
        pl.when(jnp.logical_not(diag))(lambda: step(False))

    @pl.when(j == pl.num_programs(3) - 1)
    def _():
        a0 = acc_sc[0]
        a1 = acc_sc[1]
        o = a0[:, :hd] / a0[:, hd:hd + 1] - lam_ref[0] * (a1[:, :hd] / a1[:, hd:hd + 1])
        o = o * lax.rsqrt(jnp.mean(o * o, axis=-1, keepdims=True) + EPS) * sg_ref[...] * out_scale
        o_ref[...] = o.astype(o_ref.dtype)


def _diff_attention(segs, q, kt, v, lam, subln_g, lam_init, online):
    hd = 2 * DIFF_DIM
    sigmas = jnp.asarray([LOG2E * s for s in _alibi_slopes(DIFF_HEADS)], F32)
    outs = []
    row0 = 0
    for n_seq, s_len in segs.groups:
        tq = min(1024, s_len)
        tk = min(1024, s_len)
        qb0 = row0 // tq
        kb0 = row0 // tk
        nq = s_len // tq
        nk = s_len // tk

        def kt_map(b, h, i, j, kb0=kb0, nk=nk, tq=tq, tk=tk):
            if online:
                var = KEY_DIAG
            else:
                var = jnp.where((j + 1) * tk <= i * tq, KEY_BEFORE, jnp.where(j * tk >= (i + 1) * tq, KEY_AFTER,
                                                                              KEY_DIAG))
            return (var, 0, h, 0, kb0 + b * nk + j)

        scratch = [pltpu.VMEM((2, tq, V_WIDTH), F32)]
        if online:
            scratch.append(pltpu.VMEM((2, tq, 1), F32))
        out = pl.pallas_call(
            functools.partial(_diff_kernel, tq=tq, tk=tk, out_scale=1.0 - lam_init, online=online),
            grid=(n_seq, DIFF_HEADS, nq, nk),
            in_specs=[pl.BlockSpec(memory_space=pltpu.SMEM),
                      pl.BlockSpec(memory_space=pltpu.SMEM),
                      pl.BlockSpec((2, tq, hd), lambda b, h, i, j, qb0=qb0, nq=nq: (0, qb0 + b * nq + i, h)),
                      pl.BlockSpec((None, 2, None, hd, tk), kt_map),
                      pl.BlockSpec((tk, V_WIDTH), lambda b, h, i, j, kb0=kb0, nk=nk: (kb0 + b * nk + j, h)),
                      pl.BlockSpec((1, hd), lambda b, h, i, j: (0, 0))],
            out_specs=pl.BlockSpec((tq, hd), lambda b, h, i, j, nq=nq: (b * nq + i, h)),
            out_shape=jax.ShapeDtypeStruct((n_seq * s_len, DIFF_HEADS * hd), BF16),
            scratch_shapes=scratch,
            compiler_params=_cparams(("parallel", "parallel", "parallel", "arbitrary")),
            name="diff_attention_online" if online else "diff_attention",
        )(lam, sigmas, q, kt, v, subln_g.reshape(1, hd))
        outs.append(out)
        row0 += n_seq * s_len
    return jnp.concatenate(outs, axis=0)


def _diff_mixer(segs, z, qn_g, kn_g, lam, subln_g, lam_init):
    c_shift = (LOG2E * DIFF_DIM ** 0.5) * jnp.max(jnp.abs(qn_g)) * jnp.max(jnp.abs(kn_g))
    q, kt, v = _diff_prep(segs, z, qn_g, kn_g, c_shift.reshape(1).astype(F32))
    return lax.cond(c_shift < MAX_FIXED_SHIFT,
                    lambda: _diff_attention(segs, q, kt, v, lam, subln_g, lam_init, False),
                    lambda: _diff_attention(segs, q, kt, v, lam, subln_g, lam_init, True))


def _pack_bf16_pairs(x):
    n = x.shape[1] // 2
    lo = lax.bitcast_convert_type(x[:, :n].astype(BF16).astype(F32), U32) >> 16
    hi = lax.bitcast_convert_type(x[:, n:].astype(BF16).astype(F32), U32) & jnp.uint32(0xFFFF0000)
    return hi | lo


def _unpack_bf16_pairs(w):
    lo = lax.bitcast_convert_type(w << 16, F32)
    hi = lax.bitcast_convert_type(w & jnp.uint32(0xFFFF0000), F32)
    return lo, hi


def _router_kernel(x_ref, g_ref, sh_ref, sc_ref, wt_ref, rb_ref, tri_ref, ones_ref,
                   hp_ref, idx_ref, gate_ref, rank_ref, cnt_ref, run_sc):
    @pl.when(pl.program_id(0) == 0)
    def _():
        run_sc[...] = jnp.zeros_like(run_sc)

    h = _norm_modulate(x_ref[...], g_ref[...], sh_ref[...], sc_ref[...])
    hp_ref[...] = _pack_bf16_pairs(h)
    lt = lax.dot_general(wt_ref[...], h.astype(BF16), (((1,), (1,)), ((), ())), preferred_element_type=F32)
    tm = lt.shape[1]
    scores = jax.nn.sigmoid(lt)
    biased = scores + rb_ref[...]
    neg_inf = -jnp.inf

    gsz = N_EXPERTS // N_GROUPS
    lio = lax.broadcasted_iota(I32, (gsz, tm), 0)
    grp = []
    for g in range(N_GROUPS):
        blk = biased[g * gsz:(g + 1) * gsz]
        m1 = jnp.max(blk, axis=0, keepdims=True)
        i1 = jnp.min(jnp.where(blk == m1, lio, gsz), axis=0, keepdims=True)
        m2 = jnp.max(jnp.where(lio == i1, neg_inf, blk), axis=0, keepdims=True)
        grp.append(m1 + m2)
    chosen = [jnp.zeros((1, tm), jnp.bool_) for _ in range(N_GROUPS)]
    for _ in range(TOPK_GROUPS):
        work = [jnp.where(chosen[g], neg_inf, grp[g]) for g in range(N_GROUPS)]
        mx = functools.reduce(jnp.maximum, work)
        found = jnp.zeros((1, tm), jnp.bool_)
        for g in range(N_GROUPS):
            take = (work[g] == mx) & jnp.logical_not(found)
            chosen[g] = chosen[g] | take
            found = found | take
    masked = jnp.concatenate(
        [jnp.where(chosen[g], biased[g * gsz:(g + 1) * gsz], neg_inf) for g in range(N_GROUPS)], axis=0)

    eio = lax.broadcasted_iota(I32, (N_EXPERTS, tm), 0)
    kio = lax.broadcasted_iota(I32, (TOP_K, tm), 0)
    picks, gates = [], []
    for _ in range(TOP_K):
        mx = jnp.max(masked, axis=0, keepdims=True)
        ik = jnp.min(jnp.where(masked == mx, eio, N_EXPERTS), axis=0, keepdims=True)
        hit = eio == ik
        gates.append(jnp.sum(jnp.where(hit, scores, 0.0), axis=0, keepdims=True))
        masked = jnp.where(hit, neg_inf, masked)
        picks.append(ik)
    inv = ROUTED_SCALE / functools.reduce(jnp.add, gates)

    hit_all = jnp.zeros((N_EXPERTS, tm), F32)
    for ik in picks:
        hit_all = hit_all + jnp.where(eio == ik, 1.0, 0.0)
    hit_b = hit_all.astype(BF16)
    before = jnp.dot(hit_b, tri_ref[...], preferred_element_type=F32)
    before = before + jnp.concatenate([run_sc[...]] * (tm // LANES), axis=1)
    idx_out = jnp.zeros((TOP_K, tm), I32)
    gate_out = jnp.zeros((TOP_K, tm), F32)
    rank_out = jnp.zeros((TOP_K, tm), F32)
    for k, ik in enumerate(picks):
        rk = jnp.sum(jnp.where(eio == ik, before, 0.0), axis=0, keepdims=True)
        idx_out = jnp.where(kio == k, ik, idx_out)
        gate_out = jnp.where(kio == k, gates[k] * inv, gate_out)
        rank_out = jnp.where(kio == k, rk, rank_out)
    idx_ref[...] = idx_out
    gate_ref[...] = gate_out
    rank_ref[...] = rank_out.astype(I32)
    run_sc[...] += jnp.dot(hit_b, ones_ref[...], preferred_element_type=F32)
    cnt_ref[...] = run_sc[...]


def _router(segs, x, norm, router_w, router_b):
    t, d = x.shape
    tm = min(ROUTE_ROWS, t)
    g, sh, sc = norm
    seq = lambda i: segs.seq_of_row(i * tm)
    tri = np.triu(np.ones((tm, tm), np.float32), 1)
    tok = lambda dt: jax.ShapeDtypeStruct((TOP_K, t), dt)
    tok_spec = pl.BlockSpec((TOP_K, tm), lambda i: (0, i))
    return pl.pallas_call(
        _router_kernel,
        grid=(t // tm,),
        in_specs=[pl.BlockSpec((tm, d), lambda i: (i, 0)),
                  pl.BlockSpec((1, d), lambda i: (0, 0)),
                  pl.BlockSpec((None, 1, d), lambda i: (seq(i), 0, 0)),
                  pl.BlockSpec((None, 1, d), lambda i: (seq(i), 0, 0)),
                  pl.BlockSpec((N_EXPERTS, d), lambda i: (0, 0)),
                  pl.BlockSpec((N_EXPERTS, 1), lambda i: (0, 0)),
                  pl.BlockSpec((tm, tm), lambda i: (0, 0)),
                  pl.BlockSpec((tm, LANES), lambda i: (0, 0))],
        out_specs=[pl.BlockSpec((tm, d // 2), lambda i: (i, 0)), tok_spec, tok_spec, tok_spec,
                   pl.BlockSpec((N_EXPERTS, LANES), lambda i: (0, 0))],
        out_shape=[jax.ShapeDtypeStruct((t, d // 2), U32), tok(I32), tok(F32), tok(I32),
                   jax.ShapeDtypeStruct((N_EXPERTS, LANES), F32)],
        scratch_shapes=[pltpu.VMEM((N_EXPERTS, LANES), F32)],
        compiler_params=_cparams(("arbitrary",)),
        name="moe_router",
    )(x, g.reshape(1, d), sh, sc, router_w.T.astype(BF16), router_b.astype(F32).reshape(N_EXPERTS, 1),
      jnp.asarray(tri, BF16), jnp.ones((tm, LANES), BF16))


def _row_copy(src_ref, src_row, dst_ref, dst_row, sem):
    return pltpu.make_async_copy(src_ref.at[pl.ds(src_row, 1)], dst_ref.at[pl.ds(dst_row, 1)], sem)


def _dest_kernel(idx_ref, rank_ref, start_ref, o_ref):
    tm = idx_ref.shape[1]
    eio = lax.broadcasted_iota(I32, (N_EXPERTS, tm), 0)
    for k in range(TOP_K):
        base = jnp.sum(jnp.where(eio == idx_ref[k:k + 1, :], start_ref[...], 0.0), axis=0, keepdims=True)
        o_ref[0, :, k * tm:(k + 1) * tm] = base.astype(I32) + rank_ref[k:k + 1, :]


def _dest_rows(idx, rank, pad_start, tm):
    t = idx.shape[1]
    tok_spec = pl.BlockSpec((TOP_K, tm), lambda i: (0, i))
    return pl.pallas_call(
        _dest_kernel,
        grid=(t // tm,),
        in_specs=[tok_spec, tok_spec, pl.BlockSpec((N_EXPERTS, 1), lambda i: (0, 0))],
        out_specs=pl.BlockSpec((1, 1, TOP_K * tm), lambda i: (i, 0, 0)),
        out_shape=jax.ShapeDtypeStruct((t // tm, 1, TOP_K * tm), I32),
        compiler_params=_cparams(("arbitrary",)),
        name="moe_dest_rows",
    )(idx, rank, pad_start.astype(F32).reshape(N_EXPERTS, 1))


def _dispatch_kernel(cend_ref, pend_ref, dest_ref, hp_ref, xs_ref, dest_sm, stage, zrow, sems, *, tm):
    i = pl.program_id(0)
    slot = i % 2
    dest_cp = pltpu.make_async_copy(dest_ref.at[0], dest_sm.at[pl.ds(slot, 1)], sems.at[0])
    dest_cp.start()

    @pl.when(i == 0)
    def _():
        zrow[...] = jnp.zeros_like(zrow)

        def per_expert(e, carry):
            def issue(r, c):
                _row_copy(zrow, 0, xs_ref, r, sems.at[1]).start()
                return c

            def drain(r, c):
                _row_copy(zrow, 0, xs_ref, r, sems.at[1]).wait()
                return c

            lax.fori_loop(cend_ref[e], pend_ref[e], issue, 0)
            lax.fori_loop(cend_ref[e], pend_ref[e], drain, 0)
            return carry

        lax.fori_loop(0, N_EXPERTS, per_expert, 0)

    stage[slot] = hp_ref[...]
    dest_cp.wait()

    def row_copies(s, t):
        return [_row_copy(stage.at[s], t, xs_ref, dest_sm[s, k * tm + t], sems.at[2 + s]) for k in range(TOP_K)]

    def issue(s):
        def body(t, c):
            for cp in row_copies(s, t):
                cp.start()
            return c
        lax.fori_loop(0, tm, body, 0)

    def drain(s):
        def body(t, c):
            for cp in row_copies(s, t):
                cp.wait()
            return c
        lax.fori_loop(0, tm, body, 0)

    def scatter(s):
        issue(s)
        pl.when(i > 0)(lambda: drain(1 - s))
        pl.when(i == pl.num_programs(0) - 1)(lambda: drain(s))

    pl.when(slot == 0)(lambda: scatter(0))
    pl.when(slot == 1)(lambda: scatter(1))


def _dispatch(hp, dest_tiles, cend, pend, n_rows):
    t, half = hp.shape
    tm = dest_tiles.shape[-1] // TOP_K
    grid_spec = pltpu.PrefetchScalarGridSpec(
        num_scalar_prefetch=2,
        grid=(t // tm,),
        in_specs=[pl.BlockSpec((1, 1, TOP_K * tm), lambda i, ce, pe: (i, 0, 0)),
                  pl.BlockSpec((tm, half), lambda i, ce, pe: (i, 0))],
        out_specs=pl.BlockSpec(memory_space=pl.ANY),
        scratch_shapes=[pltpu.SMEM((2, TOP_K * tm), I32), pltpu.VMEM((2, tm, half), U32),
                        pltpu.VMEM((8, half), U32), pltpu.SemaphoreType.DMA((4,))],
    )
    return pl.pallas_call(
        functools.partial(_dispatch_kernel, tm=tm),
        grid_spec=grid_spec,
        out_shape=jax.ShapeDtypeStruct((n_rows, half), U32),
        compiler_params=_cparams(("arbitrary",)),
        name="moe_dispatch",
    )(cend, pend, dest_tiles, hp)


def _expert_kernel(be_ref, first_ref, used_ref, nxt_ref, slot_ref, x_ref, w1_hbm, w3_hbm, w2_hbm, o_ref,
                   w1_buf, w3_buf, w2_buf, w1_sc, w3_sc, w2_sc, sems, *, layer):
    i = pl.program_id(0)
    pairs = ((w1_hbm, w1_buf), (w3_hbm, w3_buf), (w2_hbm, w2_buf))

    def weight_copies(e, s):
        return [pltpu.make_async_copy(w.at[layer, e], b.at[s], sems.at[n, s]) for n, (w, b) in enumerate(pairs)]

    @pl.when(i == 0)
    def _():
        for cp in weight_copies(be_ref[0], slot_ref[0]):
            cp.start()

    @pl.when(first_ref[i] == 1)
    def _():
        s = slot_ref[i]
        for cp in weight_copies(be_ref[i], s):
            cp.wait()

        @pl.when(nxt_ref[i] >= 0)
        def _():
            for cp in weight_copies(nxt_ref[i], 1 - s):
                cp.start()

        w1_sc[...] = w1_buf[s].astype(BF16)
        w3_sc[...] = w3_buf[s].astype(BF16)
        w2_sc[...] = w2_buf[s].astype(BF16)

    @pl.when(i < used_ref[0])
    def _():
        lo, hi = _unpack_bf16_pairs(x_ref[...])
        lo = lo.astype(BF16)
        hi = hi.astype(BF16)
        half = lo.shape[1]
        a = (jnp.dot(lo, w1_sc[:half], preferred_element_type=F32)
             + jnp.dot(hi, w1_sc[half:], preferred_element_type=F32))
        b = (jnp.dot(lo, w3_sc[:half], preferred_element_type=F32)
             + jnp.dot(hi, w3_sc[half:], preferred_element_type=F32))
        hb = (_silu(a) * b).astype(BF16)
        o_ref[...] = _pack_bf16_pairs(jnp.dot(hb, w2_sc[...], preferred_element_type=F32))

    @pl.when(i >= used_ref[0])
    def _():
        o_ref[...] = jnp.zeros_like(o_ref)


def _experts(xs, block_e, first, used, w1, w3, w2, layer):
    rows, half = xs.shape
    d = 2 * half
    tm = MOE_ROWS
    n_blocks = rows // tm
    de = w1.shape[-1]
    seg = jnp.cumsum(first) - 1
    seg_e = jnp.zeros((n_blocks + 1,), I32).at[seg].set(block_e)
    nxt = jnp.where(seg < seg[-1], seg_e[seg + 1], -1).astype(I32)
    slot = (seg % 2).astype(I32)
    any_spec = pl.BlockSpec(memory_space=pl.ANY)
    grid_spec = pltpu.PrefetchScalarGridSpec(
        num_scalar_prefetch=5,
        grid=(n_blocks,),
        in_specs=[pl.BlockSpec((tm, half), lambda i, *_: (i, 0)), any_spec, any_spec, any_spec],
        out_specs=pl.BlockSpec((tm, half), lambda i, *_: (i, 0)),
        scratch_shapes=[pltpu.VMEM((2, d, de), F32), pltpu.VMEM((2, d, de), F32), pltpu.VMEM((2, de, d), F32),
                        pltpu.VMEM((d, de), BF16), pltpu.VMEM((d, de), BF16), pltpu.VMEM((de, d), BF16),
                        pltpu.SemaphoreType.DMA((3, 2))],
    )
    return pl.pallas_call(
        functools.partial(_expert_kernel, layer=layer),
        grid_spec=grid_spec,
        out_shape=jax.ShapeDtypeStruct((rows, half), U32),
        compiler_params=_cparams(("arbitrary",)),
        name="swiglu_experts",
    )(block_e, first, used, nxt, slot, xs, w1, w3, w2)


def _block_plan(counts, tm, n_blocks):
    padded = (counts + tm - 1) // tm * tm
    pad_end = jnp.cumsum(padded)
    pad_start = pad_end - padded
    block_e = jnp.minimum(jnp.searchsorted(pad_end, jnp.arange(n_blocks, dtype=I32) * tm, side='right'),
                          N_EXPERTS - 1).astype(I32)
    first = jnp.concatenate([jnp.ones((1,), I32), (block_e[1:] != block_e[:-1]).astype(I32)])
    used = (pad_end[-1:] // tm).astype(I32)
    return pad_start.astype(I32), pad_end.astype(I32), block_e, first, used


def _combine_kernel(dcur_ref, dnxt_ref, x_ref, gate_ref, ysh_ref, g_ref, ys_ref, o_ref, dest_sm, buf, sems,
                    *, tm):
    i = pl.program_id(0)
    slot = i % 2

    def row_copies(s, t):
        return [_row_copy(ys_ref, dest_sm[s, k * tm + t], buf.at[s, k], t, sems.at[1 + s]) for k in range(TOP_K)]

    def gather(dref, s):
        dest_cp = pltpu.make_async_copy(dref.at[0], dest_sm.at[pl.ds(s, 1)], sems.at[0])
        dest_cp.start()
        dest_cp.wait()

        def issue(t, c):
            for cp in row_copies(s, t):
                cp.start()
            return c

        lax.fori_loop(0, tm, issue, 0)

    def drain(s):
        def body(t, c):
            for cp in row_copies(s, t):
                cp.wait()
            return c
        lax.fori_loop(0, tm, body, 0)

    def fetch(s):
        pl.when(i + 1 < pl.num_programs(0))(lambda: gather(dnxt_ref, 1 - s))
        drain(s)

    pl.when(i == 0)(lambda: gather(dcur_ref, 0))
    pl.when(slot == 0)(lambda: fetch(0))
    pl.when(slot == 1)(lambda: fetch(1))

    lo, hi = _unpack_bf16_pairs(ysh_ref[...])
    for k in range(TOP_K):
        gk = g_ref[:, k:k + 1]
        rl, rh = _unpack_bf16_pairs(buf[slot, k])
        lo = lo + gk * rl
        hi = hi + gk * rh
    half = lo.shape[1]
    o_ref[:, :half] = x_ref[:, :half] + gate_ref[:, :half] * lo
    o_ref[:, half:] = x_ref[:, half:] + gate_ref[:, half:] * hi


def _combine(segs, x, gate, ysh, ys, g_tok, dest_tiles):
    t, d = x.shape
    half = d // 2
    tm = dest_tiles.shape[-1] // TOP_K
    seq = lambda i: segs.seq_of_row(i * tm)
    n_tiles = t // tm
    return pl.pallas_call(
        functools.partial(_combine_kernel, tm=tm),
        grid=(n_tiles,),
        in_specs=[pl.BlockSpec((1, 1, TOP_K * tm), lambda i: (i, 0, 0)),
                  pl.BlockSpec((1, 1, TOP_K * tm), lambda i: (jnp.minimum(i + 1, n_tiles - 1), 0, 0)),
                  pl.BlockSpec((tm, d), lambda i: (i, 0)),
                  pl.BlockSpec((None, 1, d), lambda i: (seq(i), 0, 0)),
                  pl.BlockSpec((tm, half), lambda i: (i, 0)),
                  pl.BlockSpec((tm, TOP_K), lambda i: (i, 0)),
                  pl.BlockSpec(memory_space=pl.ANY)],
        out_specs=pl.BlockSpec((tm, d), lambda i: (i, 0)),
        out_shape=jax.ShapeDtypeStruct((t, d), F32),
        scratch_shapes=[pltpu.SMEM((2, TOP_K * tm), I32), pltpu.VMEM((2, TOP_K, tm, half), U32),
                        pltpu.SemaphoreType.DMA((3,))],
        compiler_params=_cparams(("arbitrary",)),
        name="moe_combine",
    )(dest_tiles, dest_tiles, x, gate, ysh, g_tok, ys)


def _moe_layer(segs, x, norm, gate, router_w, router_b, exp_w1, exp_w3, exp_w2, sh_w1, sh_w3, sh_w2, layer):
    t, d = x.shape
    hp, idx, g, rank, cnt = _router(segs, x, norm, router_w, router_b)
    counts = cnt[:, 0].astype(I32)
    n_blocks = (t * TOP_K) // MOE_ROWS + N_EXPERTS
    pad_start, pad_end, block_e, first, used = _block_plan(counts, MOE_ROWS, n_blocks)
    dest_tiles = _dest_rows(idx, rank, pad_start, min(ROUTE_ROWS, t))
    xs = _dispatch(hp, dest_tiles, pad_start + counts, pad_end, n_blocks * MOE_ROWS)
    ys = _experts(xs, block_e, first, used, exp_w1, exp_w3, exp_w2, layer)
    n_sh = t // MOE_ROWS
    zeros = jnp.zeros((n_sh,), I32)
    ysh = _experts(hp, zeros + layer, zeros.at[0].set(1), jnp.full((1,), n_sh, I32),
                   sh_w1[None], sh_w3[None], sh_w2[None], 0)
    return _combine(segs, x, gate, ysh, ys, g.T, dest_tiles)


def kernel(x_prompt, x_sample, c_prompt, c_sample, w_ada, b_ada, norm1_g, norm2_g, w_in_e, gm_ln_g, gm_ln_b,
           gm_ws, gm_bs, swa_qn_g, swa_kn_g, swa_sink, w_out_e, w_in_o, diff_qn_g, diff_kn_g, lam_q1, lam_k1,
           lam_q2, lam_k2, diff_subln_g, w_out_o, router_w, router_b, exp_w1, exp_w3, exp_w2, sh_w1, sh_w3,
           sh_w2):
    d = x_prompt.shape[-1]
    depth = w_ada.shape[0]
    segs = _Segs([x_prompt.shape[:2], x_sample.shape[:2]])
    n_seq = segs.n_seq
    x = jnp.concatenate([x_prompt.reshape(-1, d), x_sample.reshape(-1, d)], axis=0)

    c_rows = -(-n_seq // 16) * 16
    c_all = jnp.zeros((c_rows, d), F32).at[:n_seq].set(jnp.concatenate([c_prompt, c_sample], axis=0))
    mod = _ada(c_all, w_ada, b_ada)[:, :n_seq]

    for l in range(depth):
        sh1, sc1, g1, sh2, sc2, g2 = [m.reshape(n_seq, 1, d) for m in jnp.split(mod[l], 6, axis=-1)]
        j = l // 2
        if l % 2 == 0:
            z = _matmul(segs, x, w_in_e[j].astype(BF16), norm=(norm1_g[l], sh1, sc1))
            a = _gmlp(z, gm_ln_g[j], gm_ln_b[j], gm_ws[j], gm_bs[j])
            b = _swa(segs, z, swa_qn_g[j], swa_kn_g[j], swa_sink[j])
            mix = jnp.concatenate([a, b], axis=-1)
            x = _matmul(segs, mix, w_out_e[j].astype(BF16), resid=(x, g1))
        else:
            lam_init = 0.8 - 0.6 * math.exp(-0.3 * l)
            z = _matmul(segs, x, w_in_o[j].astype(BF16), norm=(norm1_g[l], sh1, sc1))
            lam = (jnp.exp(jnp.sum(lam_q1[j].astype(F32) * lam_k1[j].astype(F32)))
                   - jnp.exp(jnp.sum(lam_q2[j].astype(F32) * lam_k2[j].astype(F32))) + lam_init).reshape(1)
            o = _diff_mixer(segs, z, diff_qn_g[j], diff_kn_g[j], lam, diff_subln_g[j], lam_init)
            x = _matmul(segs, o, w_out_o[j].astype(BF16), resid=(x, g1))
        x = _moe_layer(segs, x, (norm2_g[l], sh2, sc2), g2, router_w[l], router_b[l], exp_w1, exp_w3, exp_w2,
                       sh_w1, sh_w3, sh_w2, l)

    t0 = x_prompt.shape[0] * x_prompt.shape[1]
    return x[:t0].reshape(x_prompt.shape), x[t0:].reshape(x_sample.shape)
```

```python
import functools
import math

import ml_dtypes
import numpy as np
import jax
import jax.numpy as jnp
from jax import lax
from jax.experimental import pallas as pl
from jax.experimental.pallas import tpu as pltpu

F32 = jnp.float32
BF16 = jnp.bfloat16
U32 = jnp.uint32
I32 = jnp.int32

BLOCK = 128
GM_GROUPS = 8
GM_CH = 128
GM_WIDTH = GM_GROUPS * GM_CH
SWA_HEADS = 8
SWA_KV = 2
HEAD_DIM = 128
WINDOW = 128
DIFF_HEADS = 16
DIFF_DIM = 64
N_EXPERTS = 256
TOP_K = 8
N_GROUPS = 8
TOPK_GROUPS = 4
D_EXPERT = 512
ROUTED_SCALE = 2.5
EPS = 1e-6
NEG_BIG = -1e30
LOG2E = 1.4426950408889634

LANES = 128
VMEM_LIMIT = 56 * 1024 * 1024
MOE_ROWS = 256
ROUTE_ROWS = 256
MAX_FIXED_SHIFT = 50.0


def _cparams(sem):
    return pltpu.CompilerParams(dimension_semantics=sem, vmem_limit_bytes=VMEM_LIMIT)


def _alibi_slopes(n):
    return [2.0 ** (-8.0 * (h + 1) / n) for h in range(n)]


def _gelu(x):
    return 0.5 * x * (1.0 + jnp.tanh(0.7978845608028654 * (x + 0.044715 * x * x * x)))


def _silu(x):
    return x * jax.nn.sigmoid(x)


class _Segs:
    def __init__(self, groups):
        self.groups = tuple(groups)
        self.tokens = sum(b * s for b, s in groups)
        self.n_seq = sum(b for b, _ in groups)

    def seq_of_row(self, r):
        out = None
        row0, seq0 = 0, 0
        for b, s in self.groups:
            val = seq0 + (r - row0) // s
            out = val if out is None else jnp.where(r >= row0, val, out)
            row0 += b * s
            seq0 += b
        return out

    def pos_in_seq(self, r):
        pos, length = None, None
        row0 = 0
        for b, s in self.groups:
            p = (r - row0) % s
            pos = p if pos is None else jnp.where(r >= row0, p, pos)
            length = s if length is None else jnp.where(r >= row0, s, length)
            row0 += b * s
        return pos, length


def _ada_kernel(c_ref, w_ref, b_ref, o_ref):
    a = _silu(c_ref[...]).astype(BF16)
    o_ref[...] = jnp.dot(a, w_ref[...].astype(BF16), preferred_element_type=F32) + b_ref[...]


def _ada(c_all, w_ada, b_ada):
    depth, d, n = w_ada.shape
    rows = c_all.shape[0]
    tn = 1024
    return pl.pallas_call(
        _ada_kernel,
        grid=(depth, n // tn),
        in_specs=[pl.BlockSpec((rows, d), lambda l, j: (0, 0)),
                  pl.BlockSpec((None, d, tn), lambda l, j: (l, 0, j)),
                  pl.BlockSpec((None, 1, tn), lambda l, j: (l, 0, j))],
        out_specs=pl.BlockSpec((None, rows, tn), lambda l, j: (l, 0, j)),
        out_shape=jax.ShapeDtypeStruct((depth, rows, n), F32),
        compiler_params=_cparams(("parallel", "parallel")),
        name="ada_modulation",
    )(c_all, w_ada, b_ada.reshape(depth, 1, n))


def _norm_modulate(x, g, shift, scale):
    y = x * lax.rsqrt(jnp.mean(x * x, axis=-1, keepdims=True) + EPS)
    return (y * g) * (1.0 + scale) + shift


def _mm_kernel(*refs, norm, resid):
    it = iter(refs)
    x_ref = next(it)
    w_ref = next(it)
    if norm:
        g_ref, sh_ref, sc_ref = next(it), next(it), next(it)
    if resid:
        res_ref, gate_ref = next(it), next(it)
    o_ref = next(it)
    h_sc = next(it) if norm else None

    if norm:
        @pl.when(pl.program_id(1) == 0)
        def _():
            h_sc[...] = _norm_modulate(x_ref[...], g_ref[...], sh_ref[...], sc_ref[...]).astype(BF16)
        lhs = h_sc[...]
    else:
        lhs = x_ref[...]
    acc = jnp.dot(lhs, w_ref[...], preferred_element_type=F32)
    if resid:
        acc = res_ref[...] + gate_ref[...] * acc
    o_ref[...] = acc.astype(o_ref.dtype)


def _matmul(segs, x, w, *, norm=None, resid=None, out_dtype=F32, tm=1024):
    m, k = x.shape
    n = w.shape[1]
    tm = min([tm] + [s for _, s in segs.groups])
    tn = next(c for c in (1024, 512, 256, n) if n % c == 0)
    seq = lambda i: segs.seq_of_row(i * tm)
    in_specs = [pl.BlockSpec((tm, k), lambda i, j: (i, 0)),
                pl.BlockSpec((k, tn), lambda i, j: (0, j))]
    args = [x, w]
    if norm is not None:
        g, sh, sc = norm
        in_specs += [pl.BlockSpec((1, k), lambda i, j: (0, 0)),
                     pl.BlockSpec((None, 1, k), lambda i, j: (seq(i), 0, 0)),
                     pl.BlockSpec((None, 1, k), lambda i, j: (seq(i), 0, 0))]
        args += [g.reshape(1, k), sh, sc]
    if resid is not None:
        res, gate = resid
        in_specs += [pl.BlockSpec((tm, tn), lambda i, j: (i, j)),
                     pl.BlockSpec((None, 1, tn), lambda i, j: (seq(i), 0, j))]
        args += [res, gate]
    scratch = [pltpu.VMEM((tm, k), BF16)] if norm is not None else []
    return pl.pallas_call(
        functools.partial(_mm_kernel, norm=norm is not None, resid=resid is not None),
        grid=(m // tm, n // tn),
        in_specs=in_specs,
        out_specs=pl.BlockSpec((tm, tn), lambda i, j: (i, j)),
        out_shape=jax.ShapeDtypeStruct((m, n), out_dtype),
        scratch_shapes=scratch,
        compiler_params=_cparams(("parallel", "arbitrary")),
        name="dense_matmul",
    )(*args)


def _gmlp_kernel(u_ref, v_ref, lng_ref, lnb_ref, ws_ref, bs_ref, o_ref, *, chunks):
    for c in range(chunks):
        rows = slice(c * BLOCK, (c + 1) * BLOCK)
        for g in range(GM_GROUPS):
            cols = slice(g * GM_CH, (g + 1) * GM_CH)
            v = _gelu(v_ref[rows, cols])
            mu = jnp.mean(v, axis=-1, keepdims=True)
            vc = v - mu
            var = jnp.mean(vc * vc, axis=-1, keepdims=True)
            vn = vc * lax.rsqrt(var + EPS) * lng_ref[:, cols] + lnb_ref[:, cols]
            s = jnp.dot(ws_ref[g], vn.astype(BF16), preferred_element_type=F32) + bs_ref[g]
            o_ref[rows, cols] = (_gelu(u_ref[rows, cols]) * s).astype(o_ref.dtype)


def _gmlp(z, ln_g, ln_b, ws, bs):
    t = z.shape[0]
    rows = min(256, t)
    return pl.pallas_call(
        functools.partial(_gmlp_kernel, chunks=rows // BLOCK),
        grid=(t // rows,),
        in_specs=[pl.BlockSpec((rows, GM_WIDTH), lambda i: (i, 0)),
                  pl.BlockSpec((rows, GM_WIDTH), lambda i: (i, 1)),
                  pl.BlockSpec((1, GM_WIDTH), lambda i: (0, 0)),
                  pl.BlockSpec((1, GM_WIDTH), lambda i: (0, 0)),
                  pl.BlockSpec((GM_GROUPS, BLOCK, BLOCK), lambda i: (0, 0, 0)),
                  pl.BlockSpec((GM_GROUPS, BLOCK, 1), lambda i: (0, 0, 0))],
        out_specs=pl.BlockSpec((rows, GM_WIDTH), lambda i: (i, 0)),
        out_shape=jax.ShapeDtypeStruct((t, GM_WIDTH), BF16),
        compiler_params=_cparams(("parallel",)),
        name="gmlp_mixer",
    )(z, z, ln_g.reshape(1, GM_WIDTH), ln_b.reshape(1, GM_WIDTH), ws.astype(BF16),
      bs.reshape(GM_GROUPS, BLOCK, 1))


def _swa_kernel(sink_ref, q_ref, kp_ref, kc_ref, kn_ref, vp_ref, vc_ref, vn_ref, qg_ref, kg_ref, o_ref,
                *, segs):
    n = pl.program_id(0)
    pos, length = segs.pos_in_seq(n * BLOCK)
    has_prev = pos > 0
    has_next = pos + BLOCK < length

    qi = lax.broadcasted_iota(jnp.int32, (BLOCK, 3 * BLOCK), 0)
    sj = lax.broadcasted_iota(jnp.int32, (BLOCK, 3 * BLOCK), 1)
    rel = jnp.abs(qi + BLOCK - sj)
    valid = rel <= WINDOW
    valid = valid & ((sj >= BLOCK) | has_prev) & ((sj < 2 * BLOCK) | has_next)
    dist = rel.astype(F32)
    slopes = _alibi_slopes(SWA_HEADS)
    group = SWA_HEADS // SWA_KV
    scale = HEAD_DIM ** -0.5

    def rms(x, g):
        return x * lax.rsqrt(jnp.mean(x * x, axis=-1, keepdims=True) + EPS) * g

    kcat = jnp.concatenate([kp_ref[...], kc_ref[...], kn_ref[...]], axis=0)
    vcat = jnp.concatenate([vp_ref[...], vc_ref[...], vn_ref[...]], axis=0).astype(BF16)
    for kh in range(SWA_KV):
        cols = slice(kh * HEAD_DIM, (kh + 1) * HEAD_DIM)
        k = rms(kcat[:, cols], kg_ref[...]).astype(BF16)
        v = vcat[:, cols]
        for gi in range(group):
            h = kh * group + gi
            hc = slice(h * HEAD_DIM, (h + 1) * HEAD_DIM)
            q = (rms(q_ref[:, hc], qg_ref[...]) * scale).astype(BF16)
            s = lax.dot_general(q, k, (((1,), (1,)), ((), ())), preferred_element_type=F32)
            s = jnp.where(valid, s - slopes[h] * dist, NEG_BIG)
            sink = sink_ref[h]
            m = jnp.maximum(jnp.max(s, axis=-1, keepdims=True), sink)
            p = jnp.exp(s - m)
            denom = jnp.sum(p, axis=-1, keepdims=True) + jnp.exp(sink - m)
            o = jnp.dot(p.astype(BF16), v, preferred_element_type=F32)
            o_ref[:, hc] = (o / denom).astype(o_ref.dtype)


def _swa(segs, z, qn_g, kn_g, sink):
    t = z.shape[0]
    nb = t // BLOCK
    qw = SWA_HEADS * HEAD_DIM
    kw = SWA_KV * HEAD_DIM
    q_blk = 2 * GM_WIDTH // qw
    k_blk = (2 * GM_WIDTH + qw) // kw
    v_blk = k_blk + 1
    prev = lambda n: jnp.maximum(n - 1, 0)
    nxt = lambda n: jnp.minimum(n + 1, nb - 1)
    return pl.pallas_call(
        functools.partial(_swa_kernel, segs=segs),
        grid=(nb,),
        in_specs=[pl.BlockSpec(memory_space=pltpu.SMEM),
                  pl.BlockSpec((BLOCK, qw), lambda n: (n, q_blk)),
                  pl.BlockSpec((BLOCK, kw), lambda n: (prev(n), k_blk)),
                  pl.BlockSpec((BLOCK, kw), lambda n: (n, k_blk)),
                  pl.BlockSpec((BLOCK, kw), lambda n: (nxt(n), k_blk)),
                  pl.BlockSpec((BLOCK, kw), lambda n: (prev(n), v_blk)),
                  pl.BlockSpec((BLOCK, kw), lambda n: (n, v_blk)),
                  pl.BlockSpec((BLOCK, kw), lambda n: (nxt(n), v_blk)),
                  pl.BlockSpec((1, HEAD_DIM), lambda n: (0, 0)),
                  pl.BlockSpec((1, HEAD_DIM), lambda n: (0, 0))],
        out_specs=pl.BlockSpec((BLOCK, qw), lambda n: (n, 0)),
        out_shape=jax.ShapeDtypeStruct((t, qw), BF16),
        compiler_params=_cparams(("parallel",)),
        name="swa_mixer",
    )(sink.astype(F32), z, z, z, z, z, z, z, qn_g.reshape(1, HEAD_DIM), kn_g.reshape(1, HEAD_DIM))


QK_ONE, QK_HI, QK_LO = DIFF_DIM, DIFF_DIM + 3, DIFF_DIM + 6
KEY_BEFORE, KEY_AFTER, KEY_DIAG = 0, 1, 2
V_WIDTH = 2 * LANES


def _split3_np(x):
    x = np.float32(x)
    p1 = np.float32(x.astype(ml_dtypes.bfloat16))
    p2 = np.float32(np.float32(x - p1).astype(ml_dtypes.bfloat16))
    p3 = np.float32(np.float32(x - p1 - p2).astype(ml_dtypes.bfloat16))
    return p1, p2, p3


def _split3(x):
    p1 = x.astype(BF16).astype(F32)
    r1 = x - p1
    p2 = r1.astype(BF16).astype(F32)
    p3 = (r1 - p2).astype(BF16).astype(F32)
    return p1, p2, p3


def _key_extra_rows():
    rows = np.zeros((2, DIFF_HEADS, 1, LANES), np.float32)
    for h, slope in enumerate(_alibi_slopes(DIFF_HEADS)):
        sigma = LOG2E * slope
        for var, sign in ((KEY_BEFORE, -1.0), (KEY_AFTER, 1.0)):
            rows[var, h, 0, QK_HI:QK_HI + 3] = _split3_np(sign * 128.0 * sigma)
            rows[var, h, 0, QK_LO:QK_LO + 3] = _split3_np(sign * sigma)
    return rows


def _diff_prep_kernel(shift_ref, z_ref, qg_ref, kg_ref, seg_ref, kx_ref, q_ref, kt_ref, v_ref, *, segs, tm):
    width = DIFF_HEADS * 2 * DIFF_DIM
    hd = 2 * DIFF_DIM
    seg = seg_ref[...]
    lane = lax.broadcasted_iota(I32, (1, hd), 1)
    body = lane < DIFF_DIM
    c_shift = shift_ref[0]

    def seg_mean_sq(x):
        x2 = x * x
        hi = x2.astype(BF16)
        lo = (x2 - hi.astype(F32)).astype(BF16)
        tot = jnp.dot(hi, seg, preferred_element_type=F32) + jnp.dot(lo, seg, preferred_element_type=F32)
        return tot * (1.0 / DIFF_DIM)

    def put3(base, first_lane, pieces):
        out = base
        for n, p in enumerate(pieces):
            out = jnp.where(lane == first_lane + n, p, out)
        return out

    pos0, _ = segs.pos_in_seq(pl.program_id(0) * tm)
    pos = pos0 + lax.broadcasted_iota(I32, (tm, 1), 0)
    posf = pos.astype(F32)
    p_hi = (pos // LANES).astype(F32)
    p_lo = (pos % LANES).astype(F32)
    q_extra = put3(put3(put3(jnp.zeros((tm, hd), F32), QK_ONE, (1.0, 1.0, 1.0)), QK_HI, (p_hi, p_hi, p_hi)),
                   QK_LO, (p_lo, p_lo, p_lo))
    diag_extra = put3(jnp.zeros((1, hd), F32), QK_ONE, _split3(jnp.full((1, 1), -c_shift, F32)))
    ones_col = jnp.where(lane == 0, 1.0, 0.0).astype(BF16)

    for h, slope in enumerate(_alibi_slopes(DIFF_HEADS)):
        sigma = LOG2E * slope
        cols = slice(h * hd, (h + 1) * hd)
        q = z_ref[:, cols]
        qn = q * lax.rsqrt(seg_mean_sq(q) + EPS) * qg_ref[...] * (LOG2E * DIFF_DIM ** -0.5)
        k = z_ref[:, width + h * hd: width + (h + 1) * hd]
        kn = k * lax.rsqrt(seg_mean_sq(k) + EPS) * kg_ref[...]
        extras = {
            KEY_BEFORE: put3(jnp.broadcast_to(kx_ref[KEY_BEFORE, h], (tm, hd)), QK_ONE,
                             _split3(sigma * posf - c_shift)),
            KEY_AFTER: put3(jnp.broadcast_to(kx_ref[KEY_AFTER, h], (tm, hd)), QK_ONE,
                            _split3(-sigma * posf - c_shift)),
            KEY_DIAG: jnp.broadcast_to(diag_extra, (tm, hd)),
        }
        for c in range(2):
            qc = qn if c == 0 else pltpu.roll(qn, DIFF_DIM, 1)
            kc = kn if c == 0 else pltpu.roll(kn, DIFF_DIM, 1)
            q_ref[c, :, cols] = jnp.where(body, qc, q_extra).astype(BF16)
            for var in (KEY_BEFORE, KEY_AFTER, KEY_DIAG):
                kt_ref[var, c, h] = jnp.where(body, kc, extras[var]).T.astype(BF16)
        v_ref[:, h * V_WIDTH: h * V_WIDTH + hd] = z_ref[:, 2 * width + h * hd: 2 * width + (h + 1) * hd].astype(BF16)
        v_ref[:, h * V_WIDTH + hd: (h + 1) * V_WIDTH] = jnp.broadcast_to(ones_col, (tm, hd))


def _diff_prep(segs, z, qn_g, kn_g, c_shift):
    t = z.shape[0]
    width = DIFF_HEADS * 2 * DIFF_DIM
    hd = 2 * DIFF_DIM
    tm = min(256, t)
    seg = np.kron(np.eye(2, dtype=np.float32), np.ones((DIFF_DIM, DIFF_DIM), np.float32))
    qg2 = jnp.concatenate([qn_g, qn_g]).reshape(1, hd)
    kg2 = jnp.concatenate([kn_g, kn_g]).reshape(1, hd)
    return pl.pallas_call(
        functools.partial(_diff_prep_kernel, segs=segs, tm=tm),
        grid=(t // tm,),
        in_specs=[pl.BlockSpec(memory_space=pltpu.SMEM),
                  pl.BlockSpec((tm, 3 * width), lambda i: (i, 0)),
                  pl.BlockSpec((1, hd), lambda i: (0, 0)),
                  pl.BlockSpec((1, hd), lambda i: (0, 0)),
                  pl.BlockSpec((hd, hd), lambda i: (0, 0)),
                  pl.BlockSpec((2, DIFF_HEADS, 1, hd), lambda i: (0, 0, 0, 0))],
        out_specs=[pl.BlockSpec((2, tm, width), lambda i: (0, i, 0)),
                   pl.BlockSpec((3, 2, DIFF_HEADS, hd, tm), lambda i: (0, 0, 0, 0, i)),
                   pl.BlockSpec((tm, DIFF_HEADS * V_WIDTH), lambda i: (i, 0))],
        out_shape=[jax.ShapeDtypeStruct((2, t, width), BF16),
                   jax.ShapeDtypeStruct((3, 2, DIFF_HEADS, hd, t), BF16),
                   jax.ShapeDtypeStruct((t, DIFF_HEADS * V_WIDTH), BF16)],
        compiler_params=_cparams(("parallel",)),
        name="diff_prep",
    )(c_shift, z, qg2, kg2, jnp.asarray(seg, BF16), jnp.asarray(_key_extra_rows()))


def _diff_kernel(lam_ref, sigma_ref, q_ref, kt_ref, v_ref, sg_ref, o_ref, acc_sc, *m_sc,
                 tq, tk, out_scale, online):
    h = pl.program_id(1)
    i = pl.program_id(2)
    j = pl.program_id(3)
    hd = 2 * DIFF_DIM

    @pl.when(j == 0)
    def _():
        acc_sc[...] = jnp.zeros_like(acc_sc)
        if online:
            m_sc[0][...] = jnp.full_like(m_sc[0], NEG_BIG)

    def step(explicit_bias):
        if explicit_bias:
            qpos = i * tq + lax.broadcasted_iota(I32, (tq, tk), 0)
            kpos = j * tk + lax.broadcasted_iota(I32, (tq, tk), 1)
            bias = sigma_ref[h] * jnp.abs(qpos - kpos).astype(F32)
        v = v_ref[...]
        for c in range(2):
            s = jnp.dot(q_ref[c], kt_ref[c], preferred_element_type=F32)
            if explicit_bias:
                s = s - bias
            if online:
                m_prev = m_sc[0][c]
                m_new = jnp.maximum(m_prev, jnp.max(s, axis=-1, keepdims=True))
                p = jnp.exp2(s - m_new)
                acc_sc[c] = jnp.exp2(m_prev - m_new) * acc_sc[c] + jnp.dot(p.astype(BF16), v,
                                                                           preferred_element_type=F32)
                m_sc[0][c] = m_new
            else:
                acc_sc[c] += jnp.dot(jnp.exp2(s).astype(BF16), v, preferred_element_type=F32)

    if online:
        step(True)
    else:
        diag = (j * tk < (i + 1) * tq) & ((j + 1) * tk > i * tq)
        pl.when(diag)(lambda: step(True))
        pl.when(jnp.logical_not(diag))(lambda: step(False))

    @pl.when(j == pl.num_programs(3) - 1)
    def _():
        a0 = acc_sc[0]
        a1 = acc_sc[1]
        o = a0[:, :hd] / a0[:, hd:hd + 1] - lam_ref[0] * (a1[:, :hd] / a1[:, hd:hd + 1])
        o = o * lax.rsqrt(jnp.mean(o * o, axis=-1, keepdims=True) + EPS) * sg_ref[...] * out_scale
        o_ref[...] = o.astype(o_ref.dtype)


def _diff_attention(segs, q, kt, v, lam, subln_g, lam_init, online):
    hd = 2 * DIFF_DIM
    sigmas = jnp.asarray([LOG2E * s for s in _alibi_slopes(DIFF_HEADS)], F32)
    outs = []
    row0 = 0
    for n_seq, s_len in segs.groups:
        tq = min(1024, s_len)
        tk = min(1024, s_len)
        qb0 = row0 // tq
        kb0 = row0 // tk
        nq = s_len // tq
        nk = s_len // tk

        def kt_map(b, h, i, j, kb0=kb0, nk=nk, tq=tq, tk=tk):
            if online:
                var = KEY_DIAG
            else:
                var = jnp.where((j + 1) * tk <= i * tq, KEY_BEFORE, jnp.where(j * tk >= (i + 1) * tq, KEY_AFTER,
                                                                              KEY_DIAG))
            return (var, 0, h, 0, kb0 + b * nk + j)

        scratch = [pltpu.VMEM((2, tq, V_WIDTH), F32)]
        if online:
            scratch.append(pltpu.VMEM((2, tq, 1), F32))
        out = pl.pallas_call(
            functools.partial(_diff_kernel, tq=tq, tk=tk, out_scale=1.0 - lam_init, online=online),
            grid=(n_seq, DIFF_HEADS, nq, nk),
            in_specs=[pl.BlockSpec(memory_space=pltpu.SMEM),
                      pl.BlockSpec(memory_space=pltpu.SMEM),
                      pl.BlockSpec((2, tq, hd), lambda b, h, i, j, qb0=qb0, nq=nq: (0, qb0 + b * nq + i, h)),
                      pl.BlockSpec((None, 2, None, hd, tk), kt_map),
                      pl.BlockSpec((tk, V_WIDTH), lambda b, h, i, j, kb0=kb0, nk=nk: (kb0 + b * nk + j, h)),
                      pl.BlockSpec((1, hd), lambda b, h, i, j: (0, 0))],
            out_specs=pl.BlockSpec((tq, hd), lambda b, h, i, j, nq=nq: (b * nq + i, h)),
            out_shape=jax.ShapeDtypeStruct((n_seq * s_len, DIFF_HEADS * hd), BF16),
            scratch_shapes=scratch,
            compiler_params=_cparams(("parallel", "parallel", "parallel", "arbitrary")),
            name="diff_attention_online" if online else "diff_attention",
        )(lam, sigmas, q, kt, v, subln_g.reshape(1, hd))
        outs.append(out)
        row0 += n_seq * s_len
    return jnp.concatenate(outs, axis=0)


def _diff_mixer(segs, z, qn_g, kn_g, lam, subln_g, lam_init):
    c_shift = (LOG2E * DIFF_DIM ** 0.5) * jnp.max(jnp.abs(qn_g)) * jnp.max(jnp.abs(kn_g))
    q, kt, v = _diff_prep(segs, z, qn_g, kn_g, c_shift.reshape(1).astype(F32))
    return lax.cond(c_shift < MAX_FIXED_SHIFT,
                    lambda: _diff_attention(segs, q, kt, v, lam, subln_g, lam_init, False),
                    lambda: _diff_attention(segs, q, kt, v, lam, subln_g, lam_init, True))


def _pack_bf16_pairs(x):
    n = x.shape[1] // 2
    lo = lax.bitcast_convert_type(x[:, :n].astype(BF16).astype(F32), U32) >> 16
    hi = lax.bitcast_convert_type(x[:, n:].astype(BF16).astype(F32), U32) & jnp.uint32(0xFFFF0000)
    return hi | lo


def _unpack_bf16_pairs(w):
    lo = lax.bitcast_convert_type(w << 16, F32)
    hi = lax.bitcast_convert_type(w & jnp.uint32(0xFFFF0000), F32)
    return lo, hi


def _router_kernel(x_ref, g_ref, sh_ref, sc_ref, wt_ref, rb_ref, tri_ref, ones_ref,
                   hp_ref, idx_ref, gate_ref, rank_ref, cnt_ref, run_sc):
    @pl.when(pl.program_id(0) == 0)
    def _():
        run_sc[...] = jnp.zeros_like(run_sc)

    h = _norm_modulate(x_ref[...], g_ref[...], sh_ref[...], sc_ref[...])
    hp_ref[...] = _pack_bf16_pairs(h)
    lt = lax.dot_general(wt_ref[...], h.astype(BF16), (((1,), (1,)), ((), ())), preferred_element_type=F32)
    tm = lt.shape[1]
    scores = jax.nn.sigmoid(lt)
    biased = scores + rb_ref[...]
    neg_inf = -jnp.inf

    gsz = N_EXPERTS // N_GROUPS
    lio = lax.broadcasted_iota(I32, (gsz, tm), 0)
    grp = []
    for g in range(N_GROUPS):
        blk = biased[g * gsz:(g + 1) * gsz]
        m1 = jnp.max(blk, axis=0, keepdims=True)
        i1 = jnp.min(jnp.where(blk == m1, lio, gsz), axis=0, keepdims=True)
        m2 = jnp.max(jnp.where(lio == i1, neg_inf, blk), axis=0, keepdims=True)
        grp.append(m1 + m2)
    chosen = [jnp.zeros((1, tm), jnp.bool_) for _ in range(N_GROUPS)]
    for _ in range(TOPK_GROUPS):
        work = [jnp.where(chosen[g], neg_inf, grp[g]) for g in range(N_GROUPS)]
        mx = functools.reduce(jnp.maximum, work)
        found = jnp.zeros((1, tm), jnp.bool_)
        for g in range(N_GROUPS):
            take = (work[g] == mx) & jnp.logical_not(found)
            chosen[g] = chosen[g] | take
            found = found | take
    masked = jnp.concatenate(
        [jnp.where(chosen[g], biased[g * gsz:(g + 1) * gsz], neg_inf) for g in range(N_GROUPS)], axis=0)

    eio = lax.broadcasted_iota(I32, (N_EXPERTS, tm), 0)
    kio = lax.broadcasted_iota(I32, (TOP_K, tm), 0)
    picks, gates = [], []
    for _ in range(TOP_K):
        mx = jnp.max(masked, axis=0, keepdims=True)
        ik = jnp.min(jnp.where(masked == mx, eio, N_EXPERTS), axis=0, keepdims=True)
        hit = eio == ik
        gates.append(jnp.sum(jnp.where(hit, scores, 0.0), axis=0, keepdims=True))
        masked = jnp.where(hit, neg_inf, masked)
        picks.append(ik)
    inv = ROUTED_SCALE / functools.reduce(jnp.add, gates)

    hit_all = jnp.zeros((N_EXPERTS, tm), F32)
    for ik in picks:
        hit_all = hit_all + jnp.where(eio == ik, 1.0, 0.0)
    hit_b = hit_all.astype(BF16)
    before = jnp.dot(hit_b, tri_ref[...], preferred_element_type=F32)
    before = before + jnp.concatenate([run_sc[...]] * (tm // LANES), axis=1)
    idx_out = jnp.zeros((TOP_K, tm), I32)
    gate_out = jnp.zeros((TOP_K, tm), F32)
    rank_out = jnp.zeros((TOP_K, tm), F32)
    for k, ik in enumerate(picks):
        rk = jnp.sum(jnp.where(eio == ik, before, 0.0), axis=0, keepdims=True)
        idx_out = jnp.where(kio == k, ik, idx_out)
        gate_out = jnp.where(kio == k, gates[k] * inv, gate_out)
        rank_out = jnp.where(kio == k, rk, rank_out)
    idx_ref[...] = idx_out
    gate_ref[...] = gate_out
    rank_ref[...] = rank_out.astype(I32)
    run_sc[...] += jnp.dot(hit_b, ones_ref[...], preferred_element_type=F32)
    cnt_ref[...] = run_sc[...]


def _router(segs, x, norm, router_w, router_b):
    t, d = x.shape
    tm = min(ROUTE_ROWS, t)
    g, sh, sc = norm
    seq = lambda i: segs.seq_of_row(i * tm)
    tri = np.triu(np.ones((tm, tm), np.float32), 1)
    tok = lambda dt: jax.ShapeDtypeStruct((TOP_K, t), dt)
    tok_spec = pl.BlockSpec((TOP_K, tm), lambda i: (0, i))
    return pl.pallas_call(
        _router_kernel,
        grid=(t // tm,),
        in_specs=[pl.BlockSpec((tm, d), lambda i: (i, 0)),
                  pl.BlockSpec((1, d), lambda i: (0, 0)),
                  pl.BlockSpec((None, 1, d), lambda i: (seq(i), 0, 0)),
                  pl.BlockSpec((None, 1, d), lambda i: (seq(i), 0, 0)),
                  pl.BlockSpec((N_EXPERTS, d), lambda i: (0, 0)),
                  pl.BlockSpec((N_EXPERTS, 1), lambda i: (0, 0)),
                  pl.BlockSpec((tm, tm), lambda i: (0, 0)),
                  pl.BlockSpec((tm, LANES), lambda i: (0, 0))],
        out_specs=[pl.BlockSpec((tm, d // 2), lambda i: (i, 0)), tok_spec, tok_spec, tok_spec,
                   pl.BlockSpec((N_EXPERTS, LANES), lambda i: (0, 0))],
        out_shape=[jax.ShapeDtypeStruct((t, d // 2), U32), tok(I32), tok(F32), tok(I32),
                   jax.ShapeDtypeStruct((N_EXPERTS, LANES), F32)],
        scratch_shapes=[pltpu.VMEM((N_EXPERTS, LANES), F32)],
        compiler_params=_cparams(("arbitrary",)),
        name="moe_router",
    )(x, g.reshape(1, d), sh, sc, router_w.T.astype(BF16), router_b.astype(F32).reshape(N_EXPERTS, 1),
      jnp.asarray(tri, BF16), jnp.ones((tm, LANES), BF16))


def _row_copy(src_ref, src_row, dst_ref, dst_row, sem):
    return pltpu.make_async_copy(src_ref.at[pl.ds(src_row, 1)], dst_ref.at[pl.ds(dst_row, 1)], sem)


def _dest_kernel(idx_ref, rank_ref, start_ref, o_ref):
    tm = idx_ref.shape[1]
    eio = lax.broadcasted_iota(I32, (N_EXPERTS, tm), 0)
    for k in range(TOP_K):
        base = jnp.sum(jnp.where(eio == idx_ref[k:k + 1, :], start_ref[...], 0.0), axis=0, keepdims=True)
        o_ref[0, :, k * tm:(k + 1) * tm] = base.astype(I32) + rank_ref[k:k + 1, :]


def _dest_rows(idx, rank, pad_start, tm):
    t = idx.shape[1]
    tok_spec = pl.BlockSpec((TOP_K, tm), lambda i: (0, i))
    return pl.pallas_call(
        _dest_kernel,
        grid=(t // tm,),
        in_specs=[tok_spec, tok_spec, pl.BlockSpec((N_EXPERTS, 1), lambda i: (0, 0))],
        out_specs=pl.BlockSpec((1, 1, TOP_K * tm), lambda i: (i, 0, 0)),
        out_shape=jax.ShapeDtypeStruct((t // tm, 1, TOP_K * tm), I32),
        compiler_params=_cparams(("arbitrary",)),
        name="moe_dest_rows",
    )(idx, rank, pad_start.astype(F32).reshape(N_EXPERTS, 1))


def _dispatch_kernel(cend_ref, pend_ref, dest_ref, hp_ref, xs_ref, dest_sm, stage, zrow, sems, *, tm):
    i = pl.program_id(0)
    slot = i % 2
    n_dest = TOP_K * tm
    dest_cp = pltpu.make_async_copy(dest_ref.at[0, 0], dest_sm.at[pl.ds(slot * n_dest, n_dest)], sems.at[0])
    dest_cp.start()

    @pl.when(i == 0)
    def _():
        zrow[...] = jnp.zeros_like(zrow)

        def per_expert(e, carry):
            aligned = (cend_ref[e] + 7) // 8 * 8

            def issue(r, c):
                _row_copy(zrow, 0, xs_ref, r, sems.at[1]).start()
                return c

            def drain(r, c):
                _row_copy(zrow, 0, xs_ref, r, sems.at[1]).wait()
                return c

            lax.fori_loop(cend_ref[e], aligned, issue, 0)
            lax.fori_loop(cend_ref[e], aligned, drain, 0)
            n_pad = pend_ref[e] - aligned
            copies = []
            row = aligned
            for b in reversed(range(3, MOE_ROWS.bit_length() - 1)):
                size = 1 << b
                bit = (n_pad >> b) & 1
                copies.append((bit, pltpu.make_async_copy(zrow.at[pl.ds(0, size)],
                                                          xs_ref.at[pl.ds(pl.multiple_of(row, 8), size)],
                                                          sems.at[1])))
                row = row + bit * size
            for bit, cp in copies:
                pl.when(bit == 1)(cp.start)
            for bit, cp in copies:
                pl.when(bit == 1)(cp.wait)
            return carry

        lax.fori_loop(0, N_EXPERTS, per_expert, 0)

    stage[slot] = hp_ref[...]
    dest_cp.wait()

    def row_copies(s, t):
        return [_row_copy(stage.at[s], t, xs_ref, dest_sm[s * n_dest + k * tm + t], sems.at[2 + s])
                for k in range(TOP_K)]

    def issue(s):
        def body(t, c):
            for cp in row_copies(s, t):
                cp.start()
            return c
        lax.fori_loop(0, tm, body, 0)

    def drain(s):
        def body(t, c):
            for cp in row_copies(s, t):
                cp.wait()
            return c
        lax.fori_loop(0, tm, body, 0)

    def scatter(s):
        issue(s)
        pl.when(i > 0)(lambda: drain(1 - s))
        pl.when(i == pl.num_programs(0) - 1)(lambda: drain(s))

    pl.when(slot == 0)(lambda: scatter(0))
    pl.when(slot == 1)(lambda: scatter(1))


def _dispatch(hp, dest_tiles, cend, pend, n_rows):
    t, half = hp.shape
    tm = dest_tiles.shape[-1] // TOP_K
    grid_spec = pltpu.PrefetchScalarGridSpec(
        num_scalar_prefetch=2,
        grid=(t // tm,),
        in_specs=[pl.BlockSpec((1, 1, TOP_K * tm), lambda i, ce, pe: (i, 0, 0)),
                  pl.BlockSpec((tm, half), lambda i, ce, pe: (i, 0))],
        out_specs=pl.BlockSpec(memory_space=pl.ANY),
        scratch_shapes=[pltpu.SMEM((2 * TOP_K * tm,), I32), pltpu.VMEM((2, tm, half), U32),
                        pltpu.VMEM((MOE_ROWS // 2, half), U32), pltpu.SemaphoreType.DMA((4,))],
    )
    return pl.pallas_call(
        functools.partial(_dispatch_kernel, tm=tm),
        grid_spec=grid_spec,
        out_shape=jax.ShapeDtypeStruct((n_rows, half), U32),
        compiler_params=_cparams(("arbitrary",)),
        name="moe_dispatch",
    )(cend, pend, dest_tiles, hp)


def _expert_kernel(be_ref, first_ref, used_ref, nxt_ref, slot_ref, x_ref, w1_hbm, w3_hbm, w2_hbm, o_ref,
                   w1_buf, w3_buf, w2_buf, w1_sc, w3_sc, w2_sc, sems, *, layer):
    i = pl.program_id(0)
    pairs = ((w1_hbm, w1_buf), (w3_hbm, w3_buf), (w2_hbm, w2_buf))

    def weight_copies(e, s):
        return [pltpu.make_async_copy(w.at[layer, e], b.at[s], sems.at[n, s]) for n, (w, b) in enumerate(pairs)]

    @pl.when(i == 0)
    def _():
        for cp in weight_copies(be_ref[0], slot_ref[0]):
            cp.start()

    @pl.when(first_ref[i] == 1)
    def _():
        s = slot_ref[i]
        for cp in weight_copies(be_ref[i], s):
            cp.wait()

        @pl.when(nxt_ref[i] >= 0)
        def _():
            for cp in weight_copies(nxt_ref[i], 1 - s):
                cp.start()

        w1_sc[...] = w1_buf[s].astype(BF16)
        w3_sc[...] = w3_buf[s].astype(BF16)
        w2_sc[...] = w2_buf[s].astype(BF16)

    @pl.when(i < used_ref[0])
    def _():
        lo, hi = _unpack_bf16_pairs(x_ref[...])
        lo = lo.astype(BF16)
        hi = hi.astype(BF16)
        half = lo.shape[1]
        a = (jnp.dot(lo, w1_sc[:half], preferred_element_type=F32)
             + jnp.dot(hi, w1_sc[half:], preferred_element_type=F32))
        b = (jnp.dot(lo, w3_sc[:half], preferred_element_type=F32)
             + jnp.dot(hi, w3_sc[half:], preferred_element_type=F32))
        hb = (_silu(a) * b).astype(BF16)
        o_ref[...] = _pack_bf16_pairs(jnp.dot(hb, w2_sc[...], preferred_element_type=F32))

    @pl.when(i >= used_ref[0])
    def _():
        o_ref[...] = jnp.zeros_like(o_ref)


def _experts(xs, block_e, first, used, w1, w3, w2, layer):
    rows, half = xs.shape
    d = 2 * half
    tm = MOE_ROWS
    n_blocks = rows // tm
    de = w1.shape[-1]
    n_e = w1.shape[1]
    eids = jnp.arange(n_e, dtype=I32)
    owns = jnp.any(block_e[:, None] == eids[None, :], axis=0)
    later = jnp.where(owns[None, :] & (eids[None, :] > block_e[:, None]), eids[None, :], n_e)
    nxt = jnp.min(later, axis=1)
    nxt = jnp.where(nxt < n_e, nxt, -1).astype(I32)
    slot = ((jnp.cumsum(first) - 1) % 2).astype(I32)
    any_spec = pl.BlockSpec(memory_space=pl.ANY)
    grid_spec = pltpu.PrefetchScalarGridSpec(
        num_scalar_prefetch=5,
        grid=(n_blocks,),
        in_specs=[pl.BlockSpec((tm, half), lambda i, *_: (i, 0)), any_spec, any_spec, any_spec],
        out_specs=pl.BlockSpec((tm, half), lambda i, *_: (i, 0)),
        scratch_shapes=[pltpu.VMEM((2, d, de), F32), pltpu.VMEM((2, d, de), F32), pltpu.VMEM((2, de, d), F32),
                        pltpu.VMEM((d, de), BF16), pltpu.VMEM((d, de), BF16), pltpu.VMEM((de, d), BF16),
                        pltpu.SemaphoreType.DMA((3, 2))],
    )
    return pl.pallas_call(
        functools.partial(_expert_kernel, layer=layer),
        grid_spec=grid_spec,
        out_shape=jax.ShapeDtypeStruct((rows, half), U32),
        compiler_params=_cparams(("arbitrary",)),
        name="swiglu_experts",
    )(block_e, first, used, nxt, slot, xs, w1, w3, w2)


def _block_plan(counts, tm, n_blocks):
    padded = (counts + tm - 1) // tm * tm
    pad_end = jnp.cumsum(padded)
    pad_start = pad_end - padded
    blk_row = jnp.arange(n_blocks, dtype=I32) * tm
    block_e = jnp.minimum(jnp.sum((pad_end[None, :] <= blk_row[:, None]).astype(I32), axis=1), N_EXPERTS - 1)
    first = jnp.concatenate([jnp.ones((1,), I32), (block_e[1:] != block_e[:-1]).astype(I32)])
    used = (pad_end[-1:] // tm).astype(I32)
    return pad_start.astype(I32), pad_end.astype(I32), block_e, first, used


def _combine_kernel(dcur_ref, dnxt_ref, x_ref, gate_ref, ysh_ref, g_ref, ys_ref, o_ref, dest_sm, buf, sems,
                    *, tm):
    i = pl.program_id(0)
    slot = i % 2
    n_dest = TOP_K * tm

    def row_copies(s, t):
        return [_row_copy(ys_ref, dest_sm[s * n_dest + k * tm + t], buf.at[s, k], t, sems.at[1 + s])
                for k in range(TOP_K)]

    def gather(dref, s):
        dest_cp = pltpu.make_async_copy(dref.at[0, 0], dest_sm.at[pl.ds(s * n_dest, n_dest)], sems.at[0])
        dest_cp.start()
        dest_cp.wait()

        def issue(t, c):
            for cp in row_copies(s, t):
                cp.start()
            return c

        lax.fori_loop(0, tm, issue, 0)

    def drain(s):
        def body(t, c):
            for cp in row_copies(s, t):
                cp.wait()
            return c
        lax.fori_loop(0, tm, body, 0)

    def fetch(s):
        pl.when(i + 1 < pl.num_programs(0))(lambda: gather(dnxt_ref, 1 - s))
        drain(s)

    pl.when(i == 0)(lambda: gather(dcur_ref, 0))
    pl.when(slot == 0)(lambda: fetch(0))
    pl.when(slot == 1)(lambda: fetch(1))

    lo, hi = _unpack_bf16_pairs(ysh_ref[...])
    for k in range(TOP_K):
        gk = g_ref[:, k:k + 1]
        rl, rh = _unpack_bf16_pairs(buf[slot, k])
        lo = lo + gk * rl
        hi = hi + gk * rh
    half = lo.shape[1]
    o_ref[:, :half] = x_ref[:, :half] + gate_ref[:, :half] * lo
    o_ref[:, half:] = x_ref[:, half:] + gate_ref[:, half:] * hi


def _combine(segs, x, gate, ysh, ys, g_tok, dest_tiles):
    t, d = x.shape
    half = d // 2
    tm = dest_tiles.shape[-1] // TOP_K
    seq = lambda i: segs.seq_of_row(i * tm)
    n_tiles = t // tm
    return pl.pallas_call(
        functools.partial(_combine_kernel, tm=tm),
        grid=(n_tiles,),
        in_specs=[pl.BlockSpec((1, 1, TOP_K * tm), lambda i: (i, 0, 0)),
                  pl.BlockSpec((1, 1, TOP_K * tm), lambda i: (jnp.minimum(i + 1, n_tiles - 1), 0, 0)),
                  pl.BlockSpec((tm, d), lambda i: (i, 0)),
                  pl.BlockSpec((None, 1, d), lambda i: (seq(i), 0, 0)),
                  pl.BlockSpec((tm, half), lambda i: (i, 0)),
                  pl.BlockSpec((tm, TOP_K), lambda i: (i, 0)),
                  pl.BlockSpec(memory_space=pl.ANY)],
        out_specs=pl.BlockSpec((tm, d), lambda i: (i, 0)),
        out_shape=jax.ShapeDtypeStruct((t, d), F32),
        scratch_shapes=[pltpu.SMEM((2 * TOP_K * tm,), I32), pltpu.VMEM((2, TOP_K, tm, half), U32),
                        pltpu.SemaphoreType.DMA((3,))],
        compiler_params=_cparams(("arbitrary",)),
        name="moe_combine",
    )(dest_tiles, dest_tiles, x, gate, ysh, g_tok, ys)


def _moe_layer(segs, x, norm, gate, router_w, router_b, exp_w1, exp_w3, exp_w2, sh_w1, sh_w3, sh_w2, layer):
    t, d = x.shape
    hp, idx, g, rank, cnt = _router(segs, x, norm, router_w, router_b)
    counts = cnt[:, 0].astype(I32)
    n_blocks = (t * TOP_K) // MOE_ROWS + N_EXPERTS
    pad_start, pad_end, block_e, first, used = _block_plan(counts, MOE_ROWS, n_blocks)
    dest_tiles = _dest_rows(idx, rank, pad_start, min(ROUTE_ROWS, t))
    xs = _dispatch(hp, dest_tiles, pad_start + counts, pad_end, n_blocks * MOE_ROWS)
    ys = _experts(xs, block_e, first, used, exp_w1, exp_w3, exp_w2, layer)
    n_sh = t // MOE_ROWS
    zeros = jnp.zeros((n_sh,), I32)
    ysh = _experts(hp, zeros + layer, zeros.at[0].set(1), jnp.full((1,), n_sh, I32),
                   sh_w1[None], sh_w3[None], sh_w2[None], 0)
    return _combine(segs, x, gate, ysh, ys, g.T, dest_tiles)


def kernel(x_prompt, x_sample, c_prompt, c_sample, w_ada, b_ada, norm1_g, norm2_g, w_in_e, gm_ln_g, gm_ln_b,
           gm_ws, gm_bs, swa_qn_g, swa_kn_g, swa_sink, w_out_e, w_in_o, diff_qn_g, diff_kn_g, lam_q1, lam_k1,
           lam_q2, lam_k2, diff_subln_g, w_out_o, router_w, router_b, exp_w1, exp_w3, exp_w2, sh_w1, sh_w3,
           sh_w2):
    d = x_prompt.shape[-1]
    depth = w_ada.shape[0]
    segs = _Segs([x_prompt.shape[:2], x_sample.shape[:2]])
    n_seq = segs.n_seq
    x = jnp.concatenate([x_prompt.reshape(-1, d), x_sample.reshape(-1, d)], axis=0)

    c_rows = -(-n_seq // 16) * 16
    c_all = jnp.zeros((c_rows, d), F32).at[:n_seq].set(jnp.concatenate([c_prompt, c_sample], axis=0))
    mod = _ada(c_all, w_ada, b_ada)[:, :n_seq]

    for l in range(depth):
        sh1, sc1, g1, sh2, sc2, g2 = [m.reshape(n_seq, 1, d) for m in jnp.split(mod[l], 6, axis=-1)]
        j = l // 2
        if l % 2 == 0:
            z = _matmul(segs, x, w_in_e[j].astype(BF16), norm=(norm1_g[l], sh1, sc1))
            a = _gmlp(z, gm_ln_g[j], gm_ln_b[j], gm_ws[j], gm_bs[j])
            b = _swa(segs, z, swa_qn_g[j], swa_kn_g[j], swa_sink[j])
            mix = jnp.concatenate([a, b], axis=-1)
            x = _matmul(segs, mix, w_out_e[j].astype(BF16), resid=(x, g1))
        else:
            lam_init = 0.8 - 0.6 * math.exp(-0.3 * l)
            z = _matmul(segs, x, w_in_o[j].astype(BF16), norm=(norm1_g[l], sh1, sc1))
            lam = (jnp.exp(jnp.sum(lam_q1[j].astype(F32) * lam_k1[j].astype(F32)))
                   - jnp.exp(jnp.sum(lam_q2[j].astype(F32) * lam_k2[j].astype(F32))) + lam_init).reshape(1)
            o = _diff_mixer(segs, z, diff_qn_g[j], diff_kn_g[j], lam, diff_subln_g[j], lam_init)
            x = _matmul(segs, o, w_out_o[j].astype(BF16), resid=(x, g1))
        x = _moe_layer(segs, x, (norm2_g[l], sh2, sc2), g2, router_w[l], router_b[l], exp_w1, exp_w3, exp_w2,
                       sh_w1, sh_w3, sh_w2, l)

    t0 = x_prompt.shape[0] * x_prompt.shape[1]
    return x[:t0].reshape(x_prompt.shape), x[t0:].reshape(x_sample.shape)
```

```python
import functools
import math

import ml_dtypes
import numpy as np
import jax
import jax.numpy as jnp
from jax import lax
from jax.experimental import pallas as pl
from jax.experimental.pallas import tpu as pltpu

F32 = jnp.float32
BF16 = jnp.bfloat16
U32 = jnp.uint32
I32 = jnp.int32

BLOCK = 128
GM_GROUPS = 8
GM_CH = 128
GM_WIDTH = GM_GROUPS * GM_CH
SWA_HEADS = 8
SWA_KV = 2
HEAD_DIM = 128
WINDOW = 128
DIFF_HEADS = 16
DIFF_DIM = 64
N_EXPERTS = 256
TOP_K = 8
N_GROUPS = 8
TOPK_GROUPS = 4
D_EXPERT = 512
ROUTED_SCALE = 2.5
EPS = 1e-6
NEG_BIG = -1e30
LOG2E = 1.4426950408889634

LANES = 128
VMEM_LIMIT = 56 * 1024 * 1024
MOE_ROWS = 256
ROUTE_ROWS = 256
MAX_FIXED_SHIFT = 50.0


def _cparams(sem):
    return pltpu.CompilerParams(dimension_semantics=sem, vmem_limit_bytes=VMEM_LIMIT)


def _alibi_slopes(n):
    return [2.0 ** (-8.0 * (h + 1) / n) for h in range(n)]


def _gelu(x):
    return 0.5 * x * (1.0 + jnp.tanh(0.7978845608028654 * (x + 0.044715 * x * x * x)))


def _silu(x):
    return x * jax.nn.sigmoid(x)


class _Segs:
    def __init__(self, groups):
        self.groups = tuple(groups)
        self.tokens = sum(b * s for b, s in groups)
        self.n_seq = sum(b for b, _ in groups)

    def seq_of_row(self, r):
        out = None
        row0, seq0 = 0, 0
        for b, s in self.groups:
            val = seq0 + (r - row0) // s
            out = val if out is None else jnp.where(r >= row0, val, out)
            row0 += b * s
            seq0 += b
        return out

    def pos_in_seq(self, r):
        pos, length = None, None
        row0 = 0
        for b, s in self.groups:
            p = (r - row0) % s
            pos = p if pos is None else jnp.where(r >= row0, p, pos)
            length = s if length is None else jnp.where(r >= row0, s, length)
            row0 += b * s
        return pos, length


def _ada_kernel(c_ref, w_ref, b_ref, o_ref):
    a = _silu(c_ref[...]).astype(BF16)
    o_ref[...] = jnp.dot(a, w_ref[...].astype(BF16), preferred_element_type=F32) + b_ref[...]


def _ada(c_all, w_ada, b_ada):
    depth, d, n = w_ada.shape
    rows = c_all.shape[0]
    tn = 1024
    return pl.pallas_call(
        _ada_kernel,
        grid=(depth, n // tn),
        in_specs=[pl.BlockSpec((rows, d), lambda l, j: (0, 0)),
                  pl.BlockSpec((None, d, tn), lambda l, j: (l, 0, j)),
                  pl.BlockSpec((None, 1, tn), lambda l, j: (l, 0, j))],
        out_specs=pl.BlockSpec((None, rows, tn), lambda l, j: (l, 0, j)),
        out_shape=jax.ShapeDtypeStruct((depth, rows, n), F32),
        compiler_params=_cparams(("parallel", "parallel")),
        name="ada_modulation",
    )(c_all, w_ada, b_ada.reshape(depth, 1, n))


def _norm_modulate(x, g, shift, scale):
    y = x * lax.rsqrt(jnp.mean(x * x, axis=-1, keepdims=True) + EPS)
    return (y * g) * (1.0 + scale) + shift


def _mm_kernel(*refs, norm, resid):
    it = iter(refs)
    x_ref = next(it)
    w_ref = next(it)
    if norm:
        g_ref, sh_ref, sc_ref = next(it), next(it), next(it)
    if resid:
        res_ref, gate_ref = next(it), next(it)
    o_ref = next(it)
    h_sc = next(it) if norm else None

    if norm:
        @pl.when(pl.program_id(1) == 0)
        def _():
            h_sc[...] = _norm_modulate(x_ref[...], g_ref[...], sh_ref[...], sc_ref[...]).astype(BF16)
        lhs = h_sc[...]
    else:
        lhs = x_ref[...]
    acc = jnp.dot(lhs, w_ref[...], preferred_element_type=F32)
    if resid:
        acc = res_ref[...] + gate_ref[...] * acc
    o_ref[...] = acc.astype(o_ref.dtype)


def _matmul(segs, x, w, *, norm=None, resid=None, out_dtype=F32, tm=1024):
    m, k = x.shape
    n = w.shape[1]
    tm = min([tm] + [s for _, s in segs.groups])
    tn = next(c for c in range(1024, 0, -LANES) if n % c == 0)
    seq = lambda i: segs.seq_of_row(i * tm)
    in_specs = [pl.BlockSpec((tm, k), lambda i, j: (i, 0)),
                pl.BlockSpec((k, tn), lambda i, j: (0, j))]
    args = [x, w]
    if norm is not None:
        g, sh, sc = norm
        in_specs += [pl.BlockSpec((1, k), lambda i, j: (0, 0)),
                     pl.BlockSpec((None, 1, k), lambda i, j: (seq(i), 0, 0)),
                     pl.BlockSpec((None, 1, k), lambda i, j: (seq(i), 0, 0))]
        args += [g.reshape(1, k), sh, sc]
    if resid is not None:
        res, gate = resid
        in_specs += [pl.BlockSpec((tm, tn), lambda i, j: (i, j)),
                     pl.BlockSpec((None, 1, tn), lambda i, j: (seq(i), 0, j))]
        args += [res, gate]
    scratch = [pltpu.VMEM((tm, k), BF16)] if norm is not None else []
    return pl.pallas_call(
        functools.partial(_mm_kernel, norm=norm is not None, resid=resid is not None),
        grid=(m // tm, n // tn),
        in_specs=in_specs,
        out_specs=pl.BlockSpec((tm, tn), lambda i, j: (i, j)),
        out_shape=jax.ShapeDtypeStruct((m, n), out_dtype),
        scratch_shapes=scratch,
        compiler_params=_cparams(("parallel", "arbitrary")),
        name="dense_matmul",
    )(*args)


def _gmlp_kernel(u_ref, v_ref, lng_ref, lnb_ref, ws_ref, bs_ref, o_ref, *, chunks):
    for c in range(chunks):
        rows = slice(c * BLOCK, (c + 1) * BLOCK)
        for g in range(GM_GROUPS):
            cols = slice(g * GM_CH, (g + 1) * GM_CH)
            v = _gelu(v_ref[rows, cols])
            mu = jnp.mean(v, axis=-1, keepdims=True)
            vc = v - mu
            var = jnp.mean(vc * vc, axis=-1, keepdims=True)
            vn = vc * lax.rsqrt(var + EPS) * lng_ref[:, cols] + lnb_ref[:, cols]
            s = jnp.dot(ws_ref[g], vn.astype(BF16), preferred_element_type=F32) + bs_ref[g]
            o_ref[rows, cols] = (_gelu(u_ref[rows, cols]) * s).astype(o_ref.dtype)


def _gmlp(z, ln_g, ln_b, ws, bs):
    t = z.shape[0]
    rows = min(256, t)
    return pl.pallas_call(
        functools.partial(_gmlp_kernel, chunks=rows // BLOCK),
        grid=(t // rows,),
        in_specs=[pl.BlockSpec((rows, GM_WIDTH), lambda i: (i, 0)),
                  pl.BlockSpec((rows, GM_WIDTH), lambda i: (i, 1)),
                  pl.BlockSpec((1, GM_WIDTH), lambda i: (0, 0)),
                  pl.BlockSpec((1, GM_WIDTH), lambda i: (0, 0)),
                  pl.BlockSpec((GM_GROUPS, BLOCK, BLOCK), lambda i: (0, 0, 0)),
                  pl.BlockSpec((GM_GROUPS, BLOCK, 1), lambda i: (0, 0, 0))],
        out_specs=pl.BlockSpec((rows, GM_WIDTH), lambda i: (i, 0)),
        out_shape=jax.ShapeDtypeStruct((t, GM_WIDTH), BF16),
        compiler_params=_cparams(("parallel",)),
        name="gmlp_mixer",
    )(z, z, ln_g.reshape(1, GM_WIDTH), ln_b.reshape(1, GM_WIDTH), ws.astype(BF16),
      bs.reshape(GM_GROUPS, BLOCK, 1))


def _swa_kernel(sink_ref, q_ref, kp_ref, kc_ref, kn_ref, vp_ref, vc_ref, vn_ref, qg_ref, kg_ref, o_ref,
                *, segs):
    n = pl.program_id(0)
    pos, length = segs.pos_in_seq(n * BLOCK)
    has_prev = pos > 0
    has_next = pos + BLOCK < length

    qi = lax.broadcasted_iota(jnp.int32, (BLOCK, 3 * BLOCK), 0)
    sj = lax.broadcasted_iota(jnp.int32, (BLOCK, 3 * BLOCK), 1)
    rel = jnp.abs(qi + BLOCK - sj)
    valid = rel <= WINDOW
    valid = valid & ((sj >= BLOCK) | has_prev) & ((sj < 2 * BLOCK) | has_next)
    dist = rel.astype(F32)
    slopes = _alibi_slopes(SWA_HEADS)
    group = SWA_HEADS // SWA_KV
    scale = HEAD_DIM ** -0.5

    def rms(x, g):
        return x * lax.rsqrt(jnp.mean(x * x, axis=-1, keepdims=True) + EPS) * g

    kcat = jnp.concatenate([kp_ref[...], kc_ref[...], kn_ref[...]], axis=0)
    vcat = jnp.concatenate([vp_ref[...], vc_ref[...], vn_ref[...]], axis=0).astype(BF16)
    for kh in range(SWA_KV):
        cols = slice(kh * HEAD_DIM, (kh + 1) * HEAD_DIM)
        k = rms(kcat[:, cols], kg_ref[...]).astype(BF16)
        v = vcat[:, cols]
        for gi in range(group):
            h = kh * group + gi
            hc = slice(h * HEAD_DIM, (h + 1) * HEAD_DIM)
            q = (rms(q_ref[:, hc], qg_ref[...]) * scale).astype(BF16)
            s = lax.dot_general(q, k, (((1,), (1,)), ((), ())), preferred_element_type=F32)
            s = jnp.where(valid, s - slopes[h] * dist, NEG_BIG)
            sink = sink_ref[h]
            m = jnp.maximum(jnp.max(s, axis=-1, keepdims=True), sink)
            p = jnp.exp(s - m)
            denom = jnp.sum(p, axis=-1, keepdims=True) + jnp.exp(sink - m)
            o = jnp.dot(p.astype(BF16), v, preferred_element_type=F32)
            o_ref[:, hc] = (o / denom).astype(o_ref.dtype)


def _swa(segs, z, qn_g, kn_g, sink):
    t = z.shape[0]
    nb = t // BLOCK
    qw = SWA_HEADS * HEAD_DIM
    kw = SWA_KV * HEAD_DIM
    q_blk = 2 * GM_WIDTH // qw
    k_blk = (2 * GM_WIDTH + qw) // kw
    v_blk = k_blk + 1
    prev = lambda n: jnp.maximum(n - 1, 0)
    nxt = lambda n: jnp.minimum(n + 1, nb - 1)
    return pl.pallas_call(
        functools.partial(_swa_kernel, segs=segs),
        grid=(nb,),
        in_specs=[pl.BlockSpec(memory_space=pltpu.SMEM),
                  pl.BlockSpec((BLOCK, qw), lambda n: (n, q_blk)),
                  pl.BlockSpec((BLOCK, kw), lambda n: (prev(n), k_blk)),
                  pl.BlockSpec((BLOCK, kw), lambda n: (n, k_blk)),
                  pl.BlockSpec((BLOCK, kw), lambda n: (nxt(n), k_blk)),
                  pl.BlockSpec((BLOCK, kw), lambda n: (prev(n), v_blk)),
                  pl.BlockSpec((BLOCK, kw), lambda n: (n, v_blk)),
                  pl.BlockSpec((BLOCK, kw), lambda n: (nxt(n), v_blk)),
                  pl.BlockSpec((1, HEAD_DIM), lambda n: (0, 0)),
                  pl.BlockSpec((1, HEAD_DIM), lambda n: (0, 0))],
        out_specs=pl.BlockSpec((BLOCK, qw), lambda n: (n, 0)),
        out_shape=jax.ShapeDtypeStruct((t, qw), BF16),
        compiler_params=_cparams(("parallel",)),
        name="swa_mixer",
    )(sink.astype(F32), z, z, z, z, z, z, z, qn_g.reshape(1, HEAD_DIM), kn_g.reshape(1, HEAD_DIM))


QK_ONE, QK_HI, QK_LO = DIFF_DIM, DIFF_DIM + 3, DIFF_DIM + 6
KEY_BEFORE, KEY_AFTER, KEY_DIAG = 0, 1, 2
V_WIDTH = 2 * LANES


def _split3_np(x):
    x = np.float32(x)
    p1 = np.float32(x.astype(ml_dtypes.bfloat16))
    p2 = np.float32(np.float32(x - p1).astype(ml_dtypes.bfloat16))
    p3 = np.float32(np.float32(x - p1 - p2).astype(ml_dtypes.bfloat16))
    return p1, p2, p3


def _split3(x):
    p1 = x.astype(BF16).astype(F32)
    r1 = x - p1
    p2 = r1.astype(BF16).astype(F32)
    p3 = (r1 - p2).astype(BF16).astype(F32)
    return p1, p2, p3


def _key_extra_rows():
    rows = np.zeros((2, DIFF_HEADS, 1, LANES), np.float32)
    for h, slope in enumerate(_alibi_slopes(DIFF_HEADS)):
        sigma = LOG2E * slope
        for var, sign in ((KEY_BEFORE, -1.0), (KEY_AFTER, 1.0)):
            rows[var, h, 0, QK_HI:QK_HI + 3] = _split3_np(sign * 128.0 * sigma)
            rows[var, h, 0, QK_LO:QK_LO + 3] = _split3_np(sign * sigma)
    return rows


def _diff_prep_kernel(shift_ref, z_ref, qg_ref, kg_ref, seg_ref, kx_ref, q_ref, kt_ref, v_ref, *, segs, tm):
    width = DIFF_HEADS * 2 * DIFF_DIM
    hd = 2 * DIFF_DIM
    seg = seg_ref[...]
    lane = lax.broadcasted_iota(I32, (1, hd), 1)
    body = lane < DIFF_DIM
    c_shift = shift_ref[0]

    def seg_mean_sq(x):
        x2 = x * x
        hi = x2.astype(BF16)
        lo = (x2 - hi.astype(F32)).astype(BF16)
        tot = jnp.dot(hi, seg, preferred_element_type=F32) + jnp.dot(lo, seg, preferred_element_type=F32)
        return tot * (1.0 / DIFF_DIM)

    def put3(base, first_lane, pieces):
        out = base
        for n, p in enumerate(pieces):
            out = jnp.where(lane == first_lane + n, p, out)
        return out

    pos0, _ = segs.pos_in_seq(pl.program_id(0) * tm)
    pos = pos0 + lax.broadcasted_iota(I32, (tm, 1), 0)
    posf = pos.astype(F32)
    p_hi = (pos // LANES).astype(F32)
    p_lo = (pos % LANES).astype(F32)
    q_extra = put3(put3(put3(jnp.zeros((tm, hd), F32), QK_ONE, (1.0, 1.0, 1.0)), QK_HI, (p_hi, p_hi, p_hi)),
                   QK_LO, (p_lo, p_lo, p_lo))
    diag_extra = put3(jnp.zeros((1, hd), F32), QK_ONE, _split3(jnp.full((1, 1), -c_shift, F32)))
    ones_col = jnp.where(lane == 0, 1.0, 0.0).astype(BF16)

    for h, slope in enumerate(_alibi_slopes(DIFF_HEADS)):
        sigma = LOG2E * slope
        cols = slice(h * hd, (h + 1) * hd)
        q = z_ref[:, cols]
        qn = q * lax.rsqrt(seg_mean_sq(q) + EPS) * qg_ref[...] * (LOG2E * DIFF_DIM ** -0.5)
        k = z_ref[:, width + h * hd: width + (h + 1) * hd]
        kn = k * lax.rsqrt(seg_mean_sq(k) + EPS) * kg_ref[...]
        extras = {
            KEY_BEFORE: put3(jnp.broadcast_to(kx_ref[KEY_BEFORE, h], (tm, hd)), QK_ONE,
                             _split3(sigma * posf - c_shift)),
            KEY_AFTER: put3(jnp.broadcast_to(kx_ref[KEY_AFTER, h], (tm, hd)), QK_ONE,
                            _split3(-sigma * posf - c_shift)),
            KEY_DIAG: jnp.broadcast_to(diag_extra, (tm, hd)),
        }
        for c in range(2):
            qc = qn if c == 0 else pltpu.roll(qn, DIFF_DIM, 1)
            kc = kn if c == 0 else pltpu.roll(kn, DIFF_DIM, 1)
            q_ref[c, :, cols] = jnp.where(body, qc, q_extra).astype(BF16)
            for var in (KEY_BEFORE, KEY_AFTER, KEY_DIAG):
                kt_ref[var, c, h] = jnp.where(body, kc, extras[var]).T.astype(BF16)
        v_ref[:, h * V_WIDTH: h * V_WIDTH + hd] = z_ref[:, 2 * width + h * hd: 2 * width + (h + 1) * hd].astype(BF16)
        v_ref[:, h * V_WIDTH + hd: (h + 1) * V_WIDTH] = jnp.broadcast_to(ones_col, (tm, hd))


def _diff_prep(segs, z, qn_g, kn_g, c_shift):
    t = z.shape[0]
    width = DIFF_HEADS * 2 * DIFF_DIM
    hd = 2 * DIFF_DIM
    tm = min(256, t)
    seg = np.kron(np.eye(2, dtype=np.float32), np.ones((DIFF_DIM, DIFF_DIM), np.float32))
    qg2 = jnp.concatenate([qn_g, qn_g]).reshape(1, hd)
    kg2 = jnp.concatenate([kn_g, kn_g]).reshape(1, hd)
    return pl.pallas_call(
        functools.partial(_diff_prep_kernel, segs=segs, tm=tm),
        grid=(t // tm,),
        in_specs=[pl.BlockSpec(memory_space=pltpu.SMEM),
                  pl.BlockSpec((tm, 3 * width), lambda i: (i, 0)),
                  pl.BlockSpec((1, hd), lambda i: (0, 0)),
                  pl.BlockSpec((1, hd), lambda i: (0, 0)),
                  pl.BlockSpec((hd, hd), lambda i: (0, 0)),
                  pl.BlockSpec((2, DIFF_HEADS, 1, hd), lambda i: (0, 0, 0, 0))],
        out_specs=[pl.BlockSpec((2, tm, width), lambda i: (0, i, 0)),
                   pl.BlockSpec((3, 2, DIFF_HEADS, hd, tm), lambda i: (0, 0, 0, 0, i)),
                   pl.BlockSpec((tm, DIFF_HEADS * V_WIDTH), lambda i: (i, 0))],
        out_shape=[jax.ShapeDtypeStruct((2, t, width), BF16),
                   jax.ShapeDtypeStruct((3, 2, DIFF_HEADS, hd, t), BF16),
                   jax.ShapeDtypeStruct((t, DIFF_HEADS * V_WIDTH), BF16)],
        compiler_params=_cparams(("parallel",)),
        name="diff_prep",
    )(c_shift, z, qg2, kg2, jnp.asarray(seg, BF16), jnp.asarray(_key_extra_rows()))


def _diff_kernel(lam_ref, sigma_ref, q_ref, kt_ref, v_ref, sg_ref, o_ref, acc_sc, *m_sc,
                 tq, tk, out_scale, online):
    h = pl.program_id(1)
    i = pl.program_id(2)
    j = pl.program_id(3)
    hd = 2 * DIFF_DIM

    @pl.when(j == 0)
    def _():
        acc_sc[...] = jnp.zeros_like(acc_sc)
        if online:
            m_sc[0][...] = jnp.full_like(m_sc[0], NEG_BIG)

    def step(explicit_bias):
        if explicit_bias:
            qpos = i * tq + lax.broadcasted_iota(I32, (tq, tk), 0)
            kpos = j * tk + lax.broadcasted_iota(I32, (tq, tk), 1)
            bias = sigma_ref[h] * jnp.abs(qpos - kpos).astype(F32)
        v = v_ref[...]
        for c in range(2):
            s = jnp.dot(q_ref[c], kt_ref[c], preferred_element_type=F32)
            if explicit_bias:
                s = s - bias
            if online:
                m_prev = m_sc[0][c]
                m_new = jnp.maximum(m_prev, jnp.max(s, axis=-1, keepdims=True))
                p = jnp.exp2(s - m_new)
                acc_sc[c] = jnp.exp2(m_prev - m_new) * acc_sc[c] + jnp.dot(p.astype(BF16), v,
                                                                           preferred_element_type=F32)
                m_sc[0][c] = m_new
            else:
                acc_sc[c] += jnp.dot(jnp.exp2(s).astype(BF16), v, preferred_element_type=F32)

    if online:
        step(True)
    else:
        diag = (j * tk < (i + 1) * tq) & ((j + 1) * tk > i * tq)
        pl.when(diag)(lambda: step(True))
        pl.when(jnp.logical_not(diag))(lambda: step(False))

    @pl.when(j == pl.num_programs(3) - 1)
    def _():
        a0 = acc_sc[0]
        a1 = acc_sc[1]
        o = a0[:, :hd] / a0[:, hd:hd + 1] - lam_ref[0] * (a1[:, :hd] / a1[:, hd:hd + 1])
        o = o * lax.rsqrt(jnp.mean(o * o, axis=-1, keepdims=True) + EPS) * sg_ref[...] * out_scale
        o_ref[...] = o.astype(o_ref.dtype)


def _diff_attention(segs, q, kt, v, lam, subln_g, lam_init, online):
    hd = 2 * DIFF_DIM
    sigmas = jnp.asarray([LOG2E * s for s in _alibi_slopes(DIFF_HEADS)], F32)
    outs = []
    row0 = 0
    for n_seq, s_len in segs.groups:
        tq = min(2048, s_len)
        tk = min(1024, s_len)
        qb0 = row0 // tq
        kb0 = row0 // tk
        nq = s_len // tq
        nk = s_len // tk

        def kt_map(b, h, i, j, kb0=kb0, nk=nk, tq=tq, tk=tk):
            if online:
                var = KEY_DIAG
            else:
                var = jnp.where((j + 1) * tk <= i * tq, KEY_BEFORE, jnp.where(j * tk >= (i + 1) * tq, KEY_AFTER,
                                                                              KEY_DIAG))
            return (var, 0, h, 0, kb0 + b * nk + j)

        scratch = [pltpu.VMEM((2, tq, V_WIDTH), F32)]
        if online:
            scratch.append(pltpu.VMEM((2, tq, 1), F32))
        out = pl.pallas_call(
            functools.partial(_diff_kernel, tq=tq, tk=tk, out_scale=1.0 - lam_init, online=online),
            grid=(n_seq, DIFF_HEADS, nq, nk),
            in_specs=[pl.BlockSpec(memory_space=pltpu.SMEM),
                      pl.BlockSpec(memory_space=pltpu.SMEM),
                      pl.BlockSpec((2, tq, hd), lambda b, h, i, j, qb0=qb0, nq=nq: (0, qb0 + b * nq + i, h)),
                      pl.BlockSpec((None, 2, None, hd, tk), kt_map),
                      pl.BlockSpec((tk, V_WIDTH), lambda b, h, i, j, kb0=kb0, nk=nk: (kb0 + b * nk + j, h)),
                      pl.BlockSpec((1, hd), lambda b, h, i, j: (0, 0))],
            out_specs=pl.BlockSpec((tq, hd), lambda b, h, i, j, nq=nq: (b * nq + i, h)),
            out_shape=jax.ShapeDtypeStruct((n_seq * s_len, DIFF_HEADS * hd), BF16),
            scratch_shapes=scratch,
            compiler_params=_cparams(("parallel", "parallel", "parallel", "arbitrary")),
            name="diff_attention_online" if online else "diff_attention",
        )(lam, sigmas, q, kt, v, subln_g.reshape(1, hd))
        outs.append(out)
        row0 += n_seq * s_len
    return jnp.concatenate(outs, axis=0)


def _diff_mixer(segs, z, qn_g, kn_g, lam, subln_g, lam_init):
    c_shift = (LOG2E * DIFF_DIM ** 0.5) * jnp.max(jnp.abs(qn_g)) * jnp.max(jnp.abs(kn_g))
    q, kt, v = _diff_prep(segs, z, qn_g, kn_g, c_shift.reshape(1).astype(F32))
    return lax.cond(c_shift < MAX_FIXED_SHIFT,
                    lambda: _diff_attention(segs, q, kt, v, lam, subln_g, lam_init, False),
                    lambda: _diff_attention(segs, q, kt, v, lam, subln_g, lam_init, True))


ROW_TILE = 8


def _load_rows(ref, n):
    return jnp.concatenate([ref[pl.ds(s, n, stride=ROW_TILE), :] for s in range(ROW_TILE)], axis=1)


def _store_rows(ref, x):
    n = x.shape[0]
    for s in range(ROW_TILE):
        ref[pl.ds(s, n, stride=ROW_TILE), :] = x[:, s * LANES:(s + 1) * LANES]


def _pack_bf16_pairs(x):
    n = x.shape[1] // 2
    lo = lax.bitcast_convert_type(x[:, :n].astype(BF16).astype(F32), U32) >> 16
    hi = lax.bitcast_convert_type(x[:, n:].astype(BF16).astype(F32), U32) & jnp.uint32(0xFFFF0000)
    return hi | lo


def _unpack_bf16_pairs(w):
    lo = lax.bitcast_convert_type(w << 16, F32)
    hi = lax.bitcast_convert_type(w & jnp.uint32(0xFFFF0000), F32)
    return lo, hi


def _router_kernel(x_ref, g_ref, sh_ref, sc_ref, wt_ref, rb_ref, tri_ref, ones_ref,
                   hp_ref, idx_ref, gate_ref, rank_ref, cnt_ref, run_sc):
    @pl.when(pl.program_id(0) == 0)
    def _():
        run_sc[...] = jnp.zeros_like(run_sc)

    h = _norm_modulate(x_ref[...], g_ref[...], sh_ref[...], sc_ref[...])
    _store_rows(hp_ref, _pack_bf16_pairs(h))
    lt = lax.dot_general(wt_ref[...], h.astype(BF16), (((1,), (1,)), ((), ())), preferred_element_type=F32)
    tm = lt.shape[1]
    scores = jax.nn.sigmoid(lt)
    biased = scores + rb_ref[...]
    neg_inf = -jnp.inf

    gsz = N_EXPERTS // N_GROUPS
    lio = lax.broadcasted_iota(I32, (gsz, tm), 0)
    grp = []
    for g in range(N_GROUPS):
        blk = biased[g * gsz:(g + 1) * gsz]
        m1 = jnp.max(blk, axis=0, keepdims=True)
        i1 = jnp.min(jnp.where(blk == m1, lio, gsz), axis=0, keepdims=True)
        m2 = jnp.max(jnp.where(lio == i1, neg_inf, blk), axis=0, keepdims=True)
        grp.append(m1 + m2)
    chosen = [jnp.zeros((1, tm), jnp.bool_) for _ in range(N_GROUPS)]
    for _ in range(TOPK_GROUPS):
        work = [jnp.where(chosen[g], neg_inf, grp[g]) for g in range(N_GROUPS)]
        mx = functools.reduce(jnp.maximum, work)
        found = jnp.zeros((1, tm), jnp.bool_)
        for g in range(N_GROUPS):
            take = (work[g] == mx) & jnp.logical_not(found)
            chosen[g] = chosen[g] | take
            found = found | take
    masked = jnp.concatenate(
        [jnp.where(chosen[g], biased[g * gsz:(g + 1) * gsz], neg_inf) for g in range(N_GROUPS)], axis=0)

    eio = lax.broadcasted_iota(I32, (N_EXPERTS, tm), 0)
    kio = lax.broadcasted_iota(I32, (TOP_K, tm), 0)
    picks, gates = [], []
    for _ in range(TOP_K):
        mx = jnp.max(masked, axis=0, keepdims=True)
        ik = jnp.min(jnp.where(masked == mx, eio, N_EXPERTS), axis=0, keepdims=True)
        hit = eio == ik
        gates.append(jnp.sum(jnp.where(hit, scores, 0.0), axis=0, keepdims=True))
        masked = jnp.where(hit, neg_inf, masked)
        picks.append(ik)
    inv = ROUTED_SCALE / functools.reduce(jnp.add, gates)

    hit_all = jnp.zeros((N_EXPERTS, tm), F32)
    for ik in picks:
        hit_all = hit_all + jnp.where(eio == ik, 1.0, 0.0)
    hit_b = hit_all.astype(BF16)
    before = jnp.dot(hit_b, tri_ref[...], preferred_element_type=F32)
    before = before + jnp.concatenate([run_sc[...]] * (tm // LANES), axis=1)
    idx_out = jnp.zeros((TOP_K, tm), I32)
    gate_out = jnp.zeros((TOP_K, tm), F32)
    rank_out = jnp.zeros((TOP_K, tm), F32)
    for k, ik in enumerate(picks):
        rk = jnp.sum(jnp.where(eio == ik, before, 0.0), axis=0, keepdims=True)
        idx_out = jnp.where(kio == k, ik, idx_out)
        gate_out = jnp.where(kio == k, gates[k] * inv, gate_out)
        rank_out = jnp.where(kio == k, rk, rank_out)
    idx_ref[...] = idx_out
    gate_ref[...] = gate_out
    rank_ref[...] = rank_out.astype(I32)
    run_sc[...] += jnp.dot(hit_b, ones_ref[...], preferred_element_type=F32)
    cnt_ref[...] = run_sc[...]


def _router(segs, x, norm, router_w, router_b):
    t, d = x.shape
    tm = min(ROUTE_ROWS, t)
    g, sh, sc = norm
    seq = lambda i: segs.seq_of_row(i * tm)
    tri = np.triu(np.ones((tm, tm), np.float32), 1)
    tok = lambda dt: jax.ShapeDtypeStruct((TOP_K, t), dt)
    tok_spec = pl.BlockSpec((TOP_K, tm), lambda i: (0, i))
    return pl.pallas_call(
        _router_kernel,
        grid=(t // tm,),
        in_specs=[pl.BlockSpec((tm, d), lambda i: (i, 0)),
                  pl.BlockSpec((1, d), lambda i: (0, 0)),
                  pl.BlockSpec((None, 1, d), lambda i: (seq(i), 0, 0)),
                  pl.BlockSpec((None, 1, d), lambda i: (seq(i), 0, 0)),
                  pl.BlockSpec((N_EXPERTS, d), lambda i: (0, 0)),
                  pl.BlockSpec((N_EXPERTS, 1), lambda i: (0, 0)),
                  pl.BlockSpec((tm, tm), lambda i: (0, 0)),
                  pl.BlockSpec((tm, LANES), lambda i: (0, 0))],
        out_specs=[pl.BlockSpec((tm * ROW_TILE, LANES), lambda i: (i, 0)), tok_spec, tok_spec, tok_spec,
                   pl.BlockSpec((N_EXPERTS, LANES), lambda i: (0, 0))],
        out_shape=[jax.ShapeDtypeStruct((t * ROW_TILE, LANES), U32), tok(I32), tok(F32), tok(I32),
                   jax.ShapeDtypeStruct((N_EXPERTS, LANES), F32)],
        scratch_shapes=[pltpu.VMEM((N_EXPERTS, LANES), F32)],
        compiler_params=_cparams(("arbitrary",)),
        name="moe_router",
    )(x, g.reshape(1, d), sh, sc, router_w.T.astype(BF16), router_b.astype(F32).reshape(N_EXPERTS, 1),
      jnp.asarray(tri, BF16), jnp.ones((tm, LANES), BF16))


def _row_tiles(ref, row, n=1):
    return ref.at[pl.ds(pl.multiple_of(row * ROW_TILE, ROW_TILE), n * ROW_TILE)]


def _row_copy(src_ref, src_row, dst_ref, dst_row, sem):
    return pltpu.make_async_copy(_row_tiles(src_ref, src_row), _row_tiles(dst_ref, dst_row), sem)


def _dest_kernel(idx_ref, rank_ref, start_ref, o_ref):
    tm = idx_ref.shape[1]
    eio = lax.broadcasted_iota(I32, (N_EXPERTS, tm), 0)
    for k in range(TOP_K):
        base = jnp.sum(jnp.where(eio == idx_ref[k:k + 1, :], start_ref[...], 0.0), axis=0, keepdims=True)
        o_ref[0, :, k * tm:(k + 1) * tm] = base.astype(I32) + rank_ref[k:k + 1, :]


def _dest_rows(idx, rank, pad_start, tm):
    t = idx.shape[1]
    tok_spec = pl.BlockSpec((TOP_K, tm), lambda i: (0, i))
    return pl.pallas_call(
        _dest_kernel,
        grid=(t // tm,),
        in_specs=[tok_spec, tok_spec, pl.BlockSpec((N_EXPERTS, 1), lambda i: (0, 0))],
        out_specs=pl.BlockSpec((1, 1, TOP_K * tm), lambda i: (i, 0, 0)),
        out_shape=jax.ShapeDtypeStruct((t // tm, 1, TOP_K * tm), I32),
        compiler_params=_cparams(("arbitrary",)),
        name="moe_dest_rows",
    )(idx, rank, pad_start.astype(F32).reshape(N_EXPERTS, 1))


def _dispatch_kernel(cend_ref, pend_ref, dest_ref, hp_ref, xs_ref, dest_sm, stage, zrow, sems, *, tm):
    i = pl.program_id(0)
    slot = i % 2
    n_dest = TOP_K * tm
    dest_cp = pltpu.make_async_copy(dest_ref.at[0, 0], dest_sm.at[pl.ds(slot * n_dest, n_dest)], sems.at[0])
    dest_cp.start()

    @pl.when(i == 0)
    def _():
        zrow[...] = jnp.zeros_like(zrow)

        def per_expert(e, carry):
            n_pad = pend_ref[e] - cend_ref[e]
            copies = []
            row = cend_ref[e]
            for b in reversed(range(MOE_ROWS.bit_length() - 1)):
                size = 1 << b
                bit = (n_pad >> b) & 1
                copies.append((bit, pltpu.make_async_copy(_row_tiles(zrow, 0, size), _row_tiles(xs_ref, row, size),
                                                          sems.at[1])))
                row = row + bit * size
            for bit, cp in copies:
                pl.when(bit == 1)(cp.start)
            for bit, cp in copies:
                pl.when(bit == 1)(cp.wait)
            return carry

        lax.fori_loop(0, N_EXPERTS, per_expert, 0)

    stage[slot] = hp_ref[...]
    dest_cp.wait()

    def row_copies(s, t):
        return [_row_copy(stage.at[s], t, xs_ref, dest_sm[s * n_dest + k * tm + t], sems.at[2 + s])
                for k in range(TOP_K)]

    def issue(s):
        def body(t, c):
            for cp in row_copies(s, t):
                cp.start()
            return c
        lax.fori_loop(0, tm, body, 0)

    def drain(s):
        def body(t, c):
            for cp in row_copies(s, t):
                cp.wait()
            return c
        lax.fori_loop(0, tm, body, 0)

    def scatter(s):
        issue(s)
        pl.when(i > 0)(lambda: drain(1 - s))
        pl.when(i == pl.num_programs(0) - 1)(lambda: drain(s))

    pl.when(slot == 0)(lambda: scatter(0))
    pl.when(slot == 1)(lambda: scatter(1))


def _dispatch(hp, dest_tiles, cend, pend, n_rows):
    t = hp.shape[0] // ROW_TILE
    tm = dest_tiles.shape[-1] // TOP_K
    grid_spec = pltpu.PrefetchScalarGridSpec(
        num_scalar_prefetch=2,
        grid=(t // tm,),
        in_specs=[pl.BlockSpec((1, 1, TOP_K * tm), lambda i, ce, pe: (i, 0, 0)),
                  pl.BlockSpec((tm * ROW_TILE, LANES), lambda i, ce, pe: (i, 0))],
        out_specs=pl.BlockSpec(memory_space=pl.ANY),
        scratch_shapes=[pltpu.SMEM((2 * TOP_K * tm,), I32), pltpu.VMEM((2, tm * ROW_TILE, LANES), U32),
                        pltpu.VMEM((MOE_ROWS // 2 * ROW_TILE, LANES), U32), pltpu.SemaphoreType.DMA((4,))],
    )
    return pl.pallas_call(
        functools.partial(_dispatch_kernel, tm=tm),
        grid_spec=grid_spec,
        out_shape=jax.ShapeDtypeStruct((n_rows * ROW_TILE, LANES), U32),
        compiler_params=_cparams(("arbitrary",)),
        name="moe_dispatch",
    )(cend, pend, dest_tiles, hp)


def _expert_kernel(be_ref, first_ref, used_ref, nxt_ref, slot_ref, x_ref, w1_hbm, w3_hbm, w2_hbm, o_ref,
                   w1_buf, w3_buf, w2_buf, w1_sc, w3_sc, w2_sc, sems, *, layer):
    i = pl.program_id(0)
    pairs = ((w1_hbm, w1_buf), (w3_hbm, w3_buf), (w2_hbm, w2_buf))

    def weight_copies(e, s):
        return [pltpu.make_async_copy(w.at[layer, e], b.at[s], sems.at[n, s]) for n, (w, b) in enumerate(pairs)]

    @pl.when(i == 0)
    def _():
        for cp in weight_copies(be_ref[0], slot_ref[0]):
            cp.start()

    @pl.when(first_ref[i] == 1)
    def _():
        s = slot_ref[i]
        for cp in weight_copies(be_ref[i], s):
            cp.wait()

        @pl.when(nxt_ref[i] >= 0)
        def _():
            for cp in weight_copies(nxt_ref[i], 1 - s):
                cp.start()

        w1_sc[...] = w1_buf[s].astype(BF16)
        w3_sc[...] = w3_buf[s].astype(BF16)
        w2_sc[...] = w2_buf[s].astype(BF16)

    @pl.when(i < used_ref[0])
    def _():
        lo, hi = _unpack_bf16_pairs(_load_rows(x_ref, MOE_ROWS))
        lo = lo.astype(BF16)
        hi = hi.astype(BF16)
        half = lo.shape[1]
        a = (jnp.dot(lo, w1_sc[:half], preferred_element_type=F32)
             + jnp.dot(hi, w1_sc[half:], preferred_element_type=F32))
        b = (jnp.dot(lo, w3_sc[:half], preferred_element_type=F32)
             + jnp.dot(hi, w3_sc[half:], preferred_element_type=F32))
        hb = (_silu(a) * b).astype(BF16)
        _store_rows(o_ref, _pack_bf16_pairs(jnp.dot(hb, w2_sc[...], preferred_element_type=F32)))

    @pl.when(i >= used_ref[0])
    def _():
        o_ref[...] = jnp.zeros_like(o_ref)


def _experts(xs, block_e, first, used, w1, w3, w2, layer):
    rows = xs.shape[0] // ROW_TILE
    d = 2 * ROW_TILE * LANES
    assert w1.shape[-2] == d, "token rows are stored as one (8, 128) word tile each"
    tm = MOE_ROWS
    n_blocks = rows // tm
    de = w1.shape[-1]
    row_spec = pl.BlockSpec((tm * ROW_TILE, LANES), lambda i, *_: (i, 0))
    n_e = w1.shape[1]
    eids = jnp.arange(n_e, dtype=I32)
    owns = jnp.any(block_e[:, None] == eids[None, :], axis=0)
    later = jnp.where(owns[None, :] & (eids[None, :] > block_e[:, None]), eids[None, :], n_e)
    nxt = jnp.min(later, axis=1)
    nxt = jnp.where(nxt < n_e, nxt, -1).astype(I32)
    slot = ((jnp.cumsum(first) - 1) % 2).astype(I32)
    any_spec = pl.BlockSpec(memory_space=pl.ANY)
    grid_spec = pltpu.PrefetchScalarGridSpec(
        num_scalar_prefetch=5,
        grid=(n_blocks,),
        in_specs=[row_spec, any_spec, any_spec, any_spec],
        out_specs=row_spec,
        scratch_shapes=[pltpu.VMEM((2, d, de), F32), pltpu.VMEM((2, d, de), F32), pltpu.VMEM((2, de, d), F32),
                        pltpu.VMEM((d, de), BF16), pltpu.VMEM((d, de), BF16), pltpu.VMEM((de, d), BF16),
                        pltpu.SemaphoreType.DMA((3, 2))],
    )
    return pl.pallas_call(
        functools.partial(_expert_kernel, layer=layer),
        grid_spec=grid_spec,
        out_shape=jax.ShapeDtypeStruct((rows * ROW_TILE, LANES), U32),
        compiler_params=_cparams(("arbitrary",)),
        name="swiglu_experts",
    )(block_e, first, used, nxt, slot, xs, w1, w3, w2)


def _block_plan(counts, tm, n_blocks):
    padded = (counts + tm - 1) // tm * tm
    pad_end = jnp.cumsum(padded)
    pad_start = pad_end - padded
    blk_row = jnp.arange(n_blocks, dtype=I32) * tm
    block_e = jnp.minimum(jnp.sum((pad_end[None, :] <= blk_row[:, None]).astype(I32), axis=1), N_EXPERTS - 1)
    first = jnp.concatenate([jnp.ones((1,), I32), (block_e[1:] != block_e[:-1]).astype(I32)])
    used = (pad_end[-1:] // tm).astype(I32)
    return pad_start.astype(I32), pad_end.astype(I32), block_e, first, used


def _combine_kernel(dcur_ref, dnxt_ref, x_ref, gate_ref, ysh_ref, g_ref, ys_ref, o_ref, dest_sm, buf, sems,
                    *, tm):
    i = pl.program_id(0)
    slot = i % 2
    n_dest = TOP_K * tm

    def row_copies(s, t):
        return [_row_copy(ys_ref, dest_sm[s * n_dest + k * tm + t], buf.at[s, k], t, sems.at[1 + s])
                for k in range(TOP_K)]

    def gather(dref, s):
        dest_cp = pltpu.make_async_copy(dref.at[0, 0], dest_sm.at[pl.ds(s * n_dest, n_dest)], sems.at[0])
        dest_cp.start()
        dest_cp.wait()

        def issue(t, c):
            for cp in row_copies(s, t):
                cp.start()
            return c

        lax.fori_loop(0, tm, issue, 0)

    def drain(s):
        def body(t, c):
            for cp in row_copies(s, t):
                cp.wait()
            return c
        lax.fori_loop(0, tm, body, 0)

    def fetch(s):
        pl.when(i + 1 < pl.num_programs(0))(lambda: gather(dnxt_ref, 1 - s))
        drain(s)

    pl.when(i == 0)(lambda: gather(dcur_ref, 0))
    pl.when(slot == 0)(lambda: fetch(0))
    pl.when(slot == 1)(lambda: fetch(1))

    lo, hi = _unpack_bf16_pairs(_load_rows(ysh_ref, tm))
    for k in range(TOP_K):
        gk = g_ref[:, k:k + 1]
        rl, rh = _unpack_bf16_pairs(_load_rows(buf.at[slot, k], tm))
        lo = lo + gk * rl
        hi = hi + gk * rh
    half = lo.shape[1]
    o_ref[:, :half] = x_ref[:, :half] + gate_ref[:, :half] * lo
    o_ref[:, half:] = x_ref[:, half:] + gate_ref[:, half:] * hi


def _combine(segs, x, gate, ysh, ys, g_tok, dest_tiles):
    t, d = x.shape
    half = d // 2
    tm = dest_tiles.shape[-1] // TOP_K
    seq = lambda i: segs.seq_of_row(i * tm)
    n_tiles = t // tm
    return pl.pallas_call(
        functools.partial(_combine_kernel, tm=tm),
        grid=(n_tiles,),
        in_specs=[pl.BlockSpec((1, 1, TOP_K * tm), lambda i: (i, 0, 0)),
                  pl.BlockSpec((1, 1, TOP_K * tm), lambda i: (jnp.minimum(i + 1, n_tiles - 1), 0, 0)),
                  pl.BlockSpec((tm, d), lambda i: (i, 0)),
                  pl.BlockSpec((None, 1, d), lambda i: (seq(i), 0, 0)),
                  pl.BlockSpec((tm * ROW_TILE, LANES), lambda i: (i, 0)),
                  pl.BlockSpec((tm, TOP_K), lambda i: (i, 0)),
                  pl.BlockSpec(memory_space=pl.ANY)],
        out_specs=pl.BlockSpec((tm, d), lambda i: (i, 0)),
        out_shape=jax.ShapeDtypeStruct((t, d), F32),
        scratch_shapes=[pltpu.SMEM((2 * TOP_K * tm,), I32), pltpu.VMEM((2, TOP_K, tm * ROW_TILE, LANES), U32),
                        pltpu.SemaphoreType.DMA((3,))],
        compiler_params=_cparams(("arbitrary",)),
        name="moe_combine",
    )(dest_tiles, dest_tiles, x, gate, ysh, g_tok, ys)


def _moe_layer(segs, x, norm, gate, router_w, router_b, exp_w1, exp_w3, exp_w2, sh_w1, sh_w3, sh_w2, layer):
    t, d = x.shape
    hp, idx, g, rank, cnt = _router(segs, x, norm, router_w, router_b)
    counts = cnt[:, 0].astype(I32)
    n_blocks = (t * TOP_K) // MOE_ROWS + N_EXPERTS
    pad_start, pad_end, block_e, first, used = _block_plan(counts, MOE_ROWS, n_blocks)
    dest_tiles = _dest_rows(idx, rank, pad_start, min(ROUTE_ROWS, t))
    xs = _dispatch(hp, dest_tiles, pad_start + counts, pad_end, n_blocks * MOE_ROWS)
    ys = _experts(xs, block_e, first, used, exp_w1, exp_w3, exp_w2, layer)
    n_sh = t // MOE_ROWS
    zeros = jnp.zeros((n_sh,), I32)
    ysh = _experts(hp, zeros + layer, zeros.at[0].set(1), jnp.full((1,), n_sh, I32),
                   sh_w1[None], sh_w3[None], sh_w2[None], 0)
    return _combine(segs, x, gate, ysh, ys, g.T, dest_tiles)


def kernel(x_prompt, x_sample, c_prompt, c_sample, w_ada, b_ada, norm1_g, norm2_g, w_in_e, gm_ln_g, gm_ln_b,
           gm_ws, gm_bs, swa_qn_g, swa_kn_g, swa_sink, w_out_e, w_in_o, diff_qn_g, diff_kn_g, lam_q1, lam_k1,
           lam_q2, lam_k2, diff_subln_g, w_out_o, router_w, router_b, exp_w1, exp_w3, exp_w2, sh_w1, sh_w3,
           sh_w2):
    d = x_prompt.shape[-1]
    depth = w_ada.shape[0]
    segs = _Segs([x_prompt.shape[:2], x_sample.shape[:2]])
    n_seq = segs.n_seq
    x = jnp.concatenate([x_prompt.reshape(-1, d), x_sample.reshape(-1, d)], axis=0)

    c_rows = -(-n_seq // 16) * 16
    c_all = jnp.zeros((c_rows, d), F32).at[:n_seq].set(jnp.concatenate([c_prompt, c_sample], axis=0))
    mod = _ada(c_all, w_ada, b_ada)[:, :n_seq]

    for l in range(depth):
        sh1, sc1, g1, sh2, sc2, g2 = [m.reshape(n_seq, 1, d) for m in jnp.split(mod[l], 6, axis=-1)]
        j = l // 2
        if l % 2 == 0:
            z = _matmul(segs, x, w_in_e[j].astype(BF16), norm=(norm1_g[l], sh1, sc1))
            a = _gmlp(z, gm_ln_g[j], gm_ln_b[j], gm_ws[j], gm_bs[j])
            b = _swa(segs, z, swa_qn_g[j], swa_kn_g[j], swa_sink[j])
            mix = jnp.concatenate([a, b], axis=-1)
            x = _matmul(segs, mix, w_out_e[j].astype(BF16), resid=(x, g1))
        else:
            lam_init = 0.8 - 0.6 * math.exp(-0.3 * l)
            z = _matmul(segs, x, w_in_o[j].astype(BF16), norm=(norm1_g[l], sh1, sc1))
            lam = (jnp.exp(jnp.sum(lam_q1[j].astype(F32) * lam_k1[j].astype(F32)))
                   - jnp.exp(jnp.sum(lam_q2[j].astype(F32) * lam_k2[j].astype(F32))) + lam_init).reshape(1)
            o = _diff_mixer(segs, z, diff_qn_g[j], diff_kn_g[j], lam, diff_subln_g[j], lam_init)
            x = _matmul(segs, o, w_out_o[j].astype(BF16), resid=(x, g1))
        x = _moe_layer(segs, x, (norm2_g[l], sh2, sc2), g2, router_w[l], router_b[l], exp_w1, exp_w3, exp_w2,
                       sh_w1, sh_w3, sh_w2, l)

    t0 = x_prompt.shape[0] * x_prompt.shape[1]
    return x[:t0].reshape(x_prompt.shape), x[t0:].reshape(x_sample.shape)
```

```python
import functools
import math

import ml_dtypes
import numpy as np
import jax
import jax.numpy as jnp
from jax import lax
from jax.experimental import pallas as pl
from jax.experimental.pallas import tpu as pltpu

F32 = jnp.float32
BF16 = jnp.bfloat16
U32 = jnp.uint32
I32 = jnp.int32

BLOCK = 128
GM_GROUPS = 8
GM_CH = 128
GM_WIDTH = GM_GROUPS * GM_CH
SWA_HEADS = 8
SWA_KV = 2
HEAD_DIM = 128
WINDOW = 128
DIFF_HEADS = 16
DIFF_DIM = 64
N_EXPERTS = 256
TOP_K = 8
N_GROUPS = 8
TOPK_GROUPS = 4
D_EXPERT = 512
ROUTED_SCALE = 2.5
EPS = 1e-6
NEG_BIG = -1e30
LOG2E = 1.4426950408889634

LANES = 128
VMEM_LIMIT = 56 * 1024 * 1024
MOE_ROWS = 256
ROUTE_ROWS = 256
MAX_FIXED_SHIFT = 50.0


def _cparams(sem):
    return pltpu.CompilerParams(dimension_semantics=sem, vmem_limit_bytes=VMEM_LIMIT)


def _alibi_slopes(n):
    return [2.0 ** (-8.0 * (h + 1) / n) for h in range(n)]


def _gelu(x):
    return 0.5 * x * (1.0 + jnp.tanh(0.7978845608028654 * (x + 0.044715 * x * x * x)))


def _silu(x):
    return x * jax.nn.sigmoid(x)


class _Segs:
    def __init__(self, groups):
        self.groups = tuple(groups)
        self.tokens = sum(b * s for b, s in groups)
        self.n_seq = sum(b for b, _ in groups)

    def seq_of_row(self, r):
        out = None
        row0, seq0 = 0, 0
        for b, s in self.groups:
            val = seq0 + (r - row0) // s
            out = val if out is None else jnp.where(r >= row0, val, out)
            row0 += b * s
            seq0 += b
        return out

    def pos_in_seq(self, r):
        pos, length = None, None
        row0 = 0
        for b, s in self.groups:
            p = (r - row0) % s
            pos = p if pos is None else jnp.where(r >= row0, p, pos)
            length = s if length is None else jnp.where(r >= row0, s, length)
            row0 += b * s
        return pos, length


def _ada_kernel(c_ref, w_ref, b_ref, o_ref):
    a = _silu(c_ref[...]).astype(BF16)
    o_ref[...] = jnp.dot(a, w_ref[...].astype(BF16), preferred_element_type=F32) + b_ref[...]


def _ada(c_all, w_ada, b_ada):
    depth, d, n = w_ada.shape
    rows = c_all.shape[0]
    tn = 1024
    return pl.pallas_call(
        _ada_kernel,
        grid=(depth, n // tn),
        in_specs=[pl.BlockSpec((rows, d), lambda l, j: (0, 0)),
                  pl.BlockSpec((None, d, tn), lambda l, j: (l, 0, j)),
                  pl.BlockSpec((None, 1, tn), lambda l, j: (l, 0, j))],
        out_specs=pl.BlockSpec((None, rows, tn), lambda l, j: (l, 0, j)),
        out_shape=jax.ShapeDtypeStruct((depth, rows, n), F32),
        compiler_params=_cparams(("parallel", "parallel")),
        name="ada_modulation",
    )(c_all, w_ada, b_ada.reshape(depth, 1, n))


def _norm_modulate(x, g, shift, scale):
    y = x * lax.rsqrt(jnp.mean(x * x, axis=-1, keepdims=True) + EPS)
    return (y * g) * (1.0 + scale) + shift


def _mm_kernel(*refs, norm, resid):
    it = iter(refs)
    x_ref = next(it)
    w_ref = next(it)
    if norm:
        g_ref, sh_ref, sc_ref = next(it), next(it), next(it)
    if resid:
        res_ref, gate_ref = next(it), next(it)
    o_ref = next(it)
    h_sc = next(it) if norm else None

    if norm:
        @pl.when(pl.program_id(1) == 0)
        def _():
            h_sc[...] = _norm_modulate(x_ref[...], g_ref[...], sh_ref[...], sc_ref[...]).astype(BF16)
        lhs = h_sc[...]
    else:
        lhs = x_ref[...]
    acc = jnp.dot(lhs, w_ref[...], preferred_element_type=F32)
    if resid:
        acc = res_ref[...] + gate_ref[...] * acc
    o_ref[...] = acc.astype(o_ref.dtype)


def _matmul(segs, x, w, *, norm=None, resid=None, out_dtype=F32, tm=1024):
    m, k = x.shape
    n = w.shape[1]
    tm = min([tm] + [s for _, s in segs.groups])
    tn = next(c for c in range(1024, 0, -LANES) if n % c == 0)
    seq = lambda i: segs.seq_of_row(i * tm)
    in_specs = [pl.BlockSpec((tm, k), lambda i, j: (i, 0)),
                pl.BlockSpec((k, tn), lambda i, j: (0, j))]
    args = [x, w]
    if norm is not None:
        g, sh, sc = norm
        in_specs += [pl.BlockSpec((1, k), lambda i, j: (0, 0)),
                     pl.BlockSpec((None, 1, k), lambda i, j: (seq(i), 0, 0)),
                     pl.BlockSpec((None, 1, k), lambda i, j: (seq(i), 0, 0))]
        args += [g.reshape(1, k), sh, sc]
    if resid is not None:
        res, gate = resid
        in_specs += [pl.BlockSpec((tm, tn), lambda i, j: (i, j)),
                     pl.BlockSpec((None, 1, tn), lambda i, j: (seq(i), 0, j))]
        args += [res, gate]
    scratch = [pltpu.VMEM((tm, k), BF16)] if norm is not None else []
    return pl.pallas_call(
        functools.partial(_mm_kernel, norm=norm is not None, resid=resid is not None),
        grid=(m // tm, n // tn),
        in_specs=in_specs,
        out_specs=pl.BlockSpec((tm, tn), lambda i, j: (i, j)),
        out_shape=jax.ShapeDtypeStruct((m, n), out_dtype),
        scratch_shapes=scratch,
        compiler_params=_cparams(("parallel", "arbitrary")),
        name="dense_matmul",
    )(*args)


def _gmlp_kernel(u_ref, v_ref, lng_ref, lnb_ref, ws_ref, bs_ref, o_ref, *, chunks):
    for c in range(chunks):
        rows = slice(c * BLOCK, (c + 1) * BLOCK)
        for g in range(GM_GROUPS):
            cols = slice(g * GM_CH, (g + 1) * GM_CH)
            v = _gelu(v_ref[rows, cols])
            mu = jnp.mean(v, axis=-1, keepdims=True)
            vc = v - mu
            var = jnp.mean(vc * vc, axis=-1, keepdims=True)
            vn = vc * lax.rsqrt(var + EPS) * lng_ref[:, cols] + lnb_ref[:, cols]
            s = jnp.dot(ws_ref[g], vn.astype(BF16), preferred_element_type=F32) + bs_ref[g]
            o_ref[rows, cols] = (_gelu(u_ref[rows, cols]) * s).astype(o_ref.dtype)


def _gmlp(z, ln_g, ln_b, ws, bs):
    t = z.shape[0]
    rows = min(256, t)
    return pl.pallas_call(
        functools.partial(_gmlp_kernel, chunks=rows // BLOCK),
        grid=(t // rows,),
        in_specs=[pl.BlockSpec((rows, GM_WIDTH), lambda i: (i, 0)),
                  pl.BlockSpec((rows, GM_WIDTH), lambda i: (i, 1)),
                  pl.BlockSpec((1, GM_WIDTH), lambda i: (0, 0)),
                  pl.BlockSpec((1, GM_WIDTH), lambda i: (0, 0)),
                  pl.BlockSpec((GM_GROUPS, BLOCK, BLOCK), lambda i: (0, 0, 0)),
                  pl.BlockSpec((GM_GROUPS, BLOCK, 1), lambda i: (0, 0, 0))],
        out_specs=pl.BlockSpec((rows, GM_WIDTH), lambda i: (i, 0)),
        out_shape=jax.ShapeDtypeStruct((t, GM_WIDTH), BF16),
        compiler_params=_cparams(("parallel",)),
        name="gmlp_mixer",
    )(z, z, ln_g.reshape(1, GM_WIDTH), ln_b.reshape(1, GM_WIDTH), ws.astype(BF16),
      bs.reshape(GM_GROUPS, BLOCK, 1))


def _swa_kernel(sink_ref, q_ref, kp_ref, kc_ref, kn_ref, vp_ref, vc_ref, vn_ref, qg_ref, kg_ref, o_ref,
                *, segs):
    n = pl.program_id(0)
    pos, length = segs.pos_in_seq(n * BLOCK)
    has_prev = pos > 0
    has_next = pos + BLOCK < length

    qi = lax.broadcasted_iota(jnp.int32, (BLOCK, 3 * BLOCK), 0)
    sj = lax.broadcasted_iota(jnp.int32, (BLOCK, 3 * BLOCK), 1)
    rel = jnp.abs(qi + BLOCK - sj)
    valid = rel <= WINDOW
    valid = valid & ((sj >= BLOCK) | has_prev) & ((sj < 2 * BLOCK) | has_next)
    dist = rel.astype(F32)
    slopes = _alibi_slopes(SWA_HEADS)
    group = SWA_HEADS // SWA_KV
    scale = HEAD_DIM ** -0.5

    def rms(x, g):
        return x * lax.rsqrt(jnp.mean(x * x, axis=-1, keepdims=True) + EPS) * g

    kcat = jnp.concatenate([kp_ref[...], kc_ref[...], kn_ref[...]], axis=0)
    vcat = jnp.concatenate([vp_ref[...], vc_ref[...], vn_ref[...]], axis=0).astype(BF16)
    for kh in range(SWA_KV):
        cols = slice(kh * HEAD_DIM, (kh + 1) * HEAD_DIM)
        k = rms(kcat[:, cols], kg_ref[...]).astype(BF16)
        v = vcat[:, cols]
        for gi in range(group):
            h = kh * group + gi
            hc = slice(h * HEAD_DIM, (h + 1) * HEAD_DIM)
            q = (rms(q_ref[:, hc], qg_ref[...]) * scale).astype(BF16)
            s = lax.dot_general(q, k, (((1,), (1,)), ((), ())), preferred_element_type=F32)
            s = jnp.where(valid, s - slopes[h] * dist, NEG_BIG)
            sink = sink_ref[h]
            m = jnp.maximum(jnp.max(s, axis=-1, keepdims=True), sink)
            p = jnp.exp(s - m)
            denom = jnp.sum(p, axis=-1, keepdims=True) + jnp.exp(sink - m)
            o = jnp.dot(p.astype(BF16), v, preferred_element_type=F32)
            o_ref[:, hc] = (o / denom).astype(o_ref.dtype)


def _swa(segs, z, qn_g, kn_g, sink):
    t = z.shape[0]
    nb = t // BLOCK
    qw = SWA_HEADS * HEAD_DIM
    kw = SWA_KV * HEAD_DIM
    q_blk = 2 * GM_WIDTH // qw
    k_blk = (2 * GM_WIDTH + qw) // kw
    v_blk = k_blk + 1
    prev = lambda n: jnp.maximum(n - 1, 0)
    nxt = lambda n: jnp.minimum(n + 1, nb - 1)
    return pl.pallas_call(
        functools.partial(_swa_kernel, segs=segs),
        grid=(nb,),
        in_specs=[pl.BlockSpec(memory_space=pltpu.SMEM),
                  pl.BlockSpec((BLOCK, qw), lambda n: (n, q_blk)),
                  pl.BlockSpec((BLOCK, kw), lambda n: (prev(n), k_blk)),
                  pl.BlockSpec((BLOCK, kw), lambda n: (n, k_blk)),
                  pl.BlockSpec((BLOCK, kw), lambda n: (nxt(n), k_blk)),
                  pl.BlockSpec((BLOCK, kw), lambda n: (prev(n), v_blk)),
                  pl.BlockSpec((BLOCK, kw), lambda n: (n, v_blk)),
                  pl.BlockSpec((BLOCK, kw), lambda n: (nxt(n), v_blk)),
                  pl.BlockSpec((1, HEAD_DIM), lambda n: (0, 0)),
                  pl.BlockSpec((1, HEAD_DIM), lambda n: (0, 0))],
        out_specs=pl.BlockSpec((BLOCK, qw), lambda n: (n, 0)),
        out_shape=jax.ShapeDtypeStruct((t, qw), BF16),
        compiler_params=_cparams(("parallel",)),
        name="swa_mixer",
    )(sink.astype(F32), z, z, z, z, z, z, z, qn_g.reshape(1, HEAD_DIM), kn_g.reshape(1, HEAD_DIM))


QK_ONE, QK_HI, QK_LO = DIFF_DIM, DIFF_DIM + 3, DIFF_DIM + 6
KEY_BEFORE, KEY_AFTER, KEY_DIAG = 0, 1, 2
V_WIDTH = 2 * LANES


def _split3_np(x):
    x = np.float32(x)
    p1 = np.float32(x.astype(ml_dtypes.bfloat16))
    p2 = np.float32(np.float32(x - p1).astype(ml_dtypes.bfloat16))
    p3 = np.float32(np.float32(x - p1 - p2).astype(ml_dtypes.bfloat16))
    return p1, p2, p3


def _split3(x):
    p1 = x.astype(BF16).astype(F32)
    r1 = x - p1
    p2 = r1.astype(BF16).astype(F32)
    p3 = (r1 - p2).astype(BF16).astype(F32)
    return p1, p2, p3


def _key_extra_rows():
    rows = np.zeros((2, DIFF_HEADS, 1, LANES), np.float32)
    for h, slope in enumerate(_alibi_slopes(DIFF_HEADS)):
        sigma = LOG2E * slope
        for var, sign in ((KEY_BEFORE, -1.0), (KEY_AFTER, 1.0)):
            rows[var, h, 0, QK_HI:QK_HI + 3] = _split3_np(sign * 128.0 * sigma)
            rows[var, h, 0, QK_LO:QK_LO + 3] = _split3_np(sign * sigma)
    return rows


def _diff_prep_kernel(shift_ref, z_ref, qg_ref, kg_ref, seg_ref, kx_ref, q_ref, kt_ref, v_ref, *, segs, tm):
    width = DIFF_HEADS * 2 * DIFF_DIM
    hd = 2 * DIFF_DIM
    seg = seg_ref[...]
    lane = lax.broadcasted_iota(I32, (1, hd), 1)
    body = lane < DIFF_DIM
    c_shift = shift_ref[0]

    def seg_mean_sq(x):
        x2 = x * x
        hi = x2.astype(BF16)
        lo = (x2 - hi.astype(F32)).astype(BF16)
        tot = jnp.dot(hi, seg, preferred_element_type=F32) + jnp.dot(lo, seg, preferred_element_type=F32)
        return tot * (1.0 / DIFF_DIM)

    def put3(base, first_lane, pieces):
        out = base
        for n, p in enumerate(pieces):
            out = jnp.where(lane == first_lane + n, p, out)
        return out

    pos0, _ = segs.pos_in_seq(pl.program_id(0) * tm)
    pos = pos0 + lax.broadcasted_iota(I32, (tm, 1), 0)
    posf = pos.astype(F32)
    p_hi = (pos // LANES).astype(F32)
    p_lo = (pos % LANES).astype(F32)
    q_extra = put3(put3(put3(jnp.zeros((tm, hd), F32), QK_ONE, (1.0, 1.0, 1.0)), QK_HI, (p_hi, p_hi, p_hi)),
                   QK_LO, (p_lo, p_lo, p_lo))
    diag_extra = put3(jnp.zeros((1, hd), F32), QK_ONE, _split3(jnp.full((1, 1), -c_shift, F32)))
    ones_col = jnp.where(lane == 0, 1.0, 0.0).astype(BF16)

    for h, slope in enumerate(_alibi_slopes(DIFF_HEADS)):
        sigma = LOG2E * slope
        cols = slice(h * hd, (h + 1) * hd)
        q = z_ref[:, cols]
        qn = q * lax.rsqrt(seg_mean_sq(q) + EPS) * qg_ref[...] * (LOG2E * DIFF_DIM ** -0.5)
        k = z_ref[:, width + h * hd: width + (h + 1) * hd]
        kn = k * lax.rsqrt(seg_mean_sq(k) + EPS) * kg_ref[...]
        extras = {
            KEY_BEFORE: put3(jnp.broadcast_to(kx_ref[KEY_BEFORE, h], (tm, hd)), QK_ONE,
                             _split3(sigma * posf - c_shift)),
            KEY_AFTER: put3(jnp.broadcast_to(kx_ref[KEY_AFTER, h], (tm, hd)), QK_ONE,
                            _split3(-sigma * posf - c_shift)),
            KEY_DIAG: jnp.broadcast_to(diag_extra, (tm, hd)),
        }
        for c in range(2):
            qc = qn if c == 0 else pltpu.roll(qn, DIFF_DIM, 1)
            kc = kn if c == 0 else pltpu.roll(kn, DIFF_DIM, 1)
            q_ref[c, :, cols] = jnp.where(body, qc, q_extra).astype(BF16)
            for var in (KEY_BEFORE, KEY_AFTER, KEY_DIAG):
                kt_ref[var, c, h] = jnp.where(body, kc, extras[var]).T.astype(BF16)
        v_ref[:, h * V_WIDTH: h * V_WIDTH + hd] = z_ref[:, 2 * width + h * hd: 2 * width + (h + 1) * hd].astype(BF16)
        v_ref[:, h * V_WIDTH + hd: (h + 1) * V_WIDTH] = jnp.broadcast_to(ones_col, (tm, hd))


def _diff_prep(segs, z, qn_g, kn_g, c_shift):
    t = z.shape[0]
    width = DIFF_HEADS * 2 * DIFF_DIM
    hd = 2 * DIFF_DIM
    tm = min(256, t)
    seg = np.kron(np.eye(2, dtype=np.float32), np.ones((DIFF_DIM, DIFF_DIM), np.float32))
    qg2 = jnp.concatenate([qn_g, qn_g]).reshape(1, hd)
    kg2 = jnp.concatenate([kn_g, kn_g]).reshape(1, hd)
    return pl.pallas_call(
        functools.partial(_diff_prep_kernel, segs=segs, tm=tm),
        grid=(t // tm,),
        in_specs=[pl.BlockSpec(memory_space=pltpu.SMEM),
                  pl.BlockSpec((tm, 3 * width), lambda i: (i, 0)),
                  pl.BlockSpec((1, hd), lambda i: (0, 0)),
                  pl.BlockSpec((1, hd), lambda i: (0, 0)),
                  pl.BlockSpec((hd, hd), lambda i: (0, 0)),
                  pl.BlockSpec((2, DIFF_HEADS, 1, hd), lambda i: (0, 0, 0, 0))],
        out_specs=[pl.BlockSpec((2, tm, width), lambda i: (0, i, 0)),
                   pl.BlockSpec((3, 2, DIFF_HEADS, hd, tm), lambda i: (0, 0, 0, 0, i)),
                   pl.BlockSpec((tm, DIFF_HEADS * V_WIDTH), lambda i: (i, 0))],
        out_shape=[jax.ShapeDtypeStruct((2, t, width), BF16),
                   jax.ShapeDtypeStruct((3, 2, DIFF_HEADS, hd, t), BF16),
                   jax.ShapeDtypeStruct((t, DIFF_HEADS * V_WIDTH), BF16)],
        compiler_params=_cparams(("parallel",)),
        name="diff_prep",
    )(c_shift, z, qg2, kg2, jnp.asarray(seg, BF16), jnp.asarray(_key_extra_rows()))


def _diff_kernel(lam_ref, sigma_ref, q_ref, kt_ref, v_ref, sg_ref, o_ref, acc_sc, *m_sc,
                 tq, tk, out_scale, online):
    h = pl.program_id(1)
    i = pl.program_id(2)
    j = pl.program_id(3)
    hd = 2 * DIFF_DIM

    @pl.when(j == 0)
    def _():
        acc_sc[...] = jnp.zeros_like(acc_sc)
        if online:
            m_sc[0][...] = jnp.full_like(m_sc[0], NEG_BIG)

    def step(explicit_bias):
        if explicit_bias:
            qpos = i * tq + lax.broadcasted_iota(I32, (tq, tk), 0)
            kpos = j * tk + lax.broadcasted_iota(I32, (tq, tk), 1)
            bias = sigma_ref[h] * jnp.abs(qpos - kpos).astype(F32)
        v = v_ref[...]
        for c in range(2):
            s = jnp.dot(q_ref[c], kt_ref[c], preferred_element_type=F32)
            if explicit_bias:
                s = s - bias
            if online:
                m_prev = m_sc[0][c]
                m_new = jnp.maximum(m_prev, jnp.max(s, axis=-1, keepdims=True))
                p = jnp.exp2(s - m_new)
                acc_sc[c] = jnp.exp2(m_prev - m_new) * acc_sc[c] + jnp.dot(p.astype(BF16), v,
                                                                           preferred_element_type=F32)
                m_sc[0][c] = m_new
            else:
                acc_sc[c] += jnp.dot(jnp.exp2(s).astype(BF16), v, preferred_element_type=F32)

    if online:
        step(True)
    else:
        diag = (j * tk < (i + 1) * tq) & ((j + 1) * tk > i * tq)
        pl.when(diag)(lambda: step(True))
        pl.when(jnp.logical_not(diag))(lambda: step(False))

    @pl.when(j == pl.num_programs(3) - 1)
    def _():
        a0 = acc_sc[0]
        a1 = acc_sc[1]
        o = a0[:, :hd] / a0[:, hd:hd + 1] - lam_ref[0] * (a1[:, :hd] / a1[:, hd:hd + 1])
        o = o * lax.rsqrt(jnp.mean(o * o, axis=-1, keepdims=True) + EPS) * sg_ref[...] * out_scale
        o_ref[...] = o.astype(o_ref.dtype)


def _diff_attention(segs, q, kt, v, lam, subln_g, lam_init, online):
    hd = 2 * DIFF_DIM
    sigmas = jnp.asarray([LOG2E * s for s in _alibi_slopes(DIFF_HEADS)], F32)
    outs = []
    row0 = 0
    for n_seq, s_len in segs.groups:
        tq = min(2048, s_len)
        tk = min(1024, s_len)
        qb0 = row0 // tq
        kb0 = row0 // tk
        nq = s_len // tq
        nk = s_len // tk

        def kt_map(b, h, i, j, kb0=kb0, nk=nk, tq=tq, tk=tk):
            if online:
                var = KEY_DIAG
            else:
                var = jnp.where((j + 1) * tk <= i * tq, KEY_BEFORE, jnp.where(j * tk >= (i + 1) * tq, KEY_AFTER,
                                                                              KEY_DIAG))
            return (var, 0, h, 0, kb0 + b * nk + j)

        scratch = [pltpu.VMEM((2, tq, V_WIDTH), F32)]
        if online:
            scratch.append(pltpu.VMEM((2, tq, 1), F32))
        out = pl.pallas_call(
            functools.partial(_diff_kernel, tq=tq, tk=tk, out_scale=1.0 - lam_init, online=online),
            grid=(n_seq, DIFF_HEADS, nq, nk),
            in_specs=[pl.BlockSpec(memory_space=pltpu.SMEM),
                      pl.BlockSpec(memory_space=pltpu.SMEM),
                      pl.BlockSpec((2, tq, hd), lambda b, h, i, j, qb0=qb0, nq=nq: (0, qb0 + b * nq + i, h)),
                      pl.BlockSpec((None, 2, None, hd, tk), kt_map),
                      pl.BlockSpec((tk, V_WIDTH), lambda b, h, i, j, kb0=kb0, nk=nk: (kb0 + b * nk + j, h)),
                      pl.BlockSpec((1, hd), lambda b, h, i, j: (0, 0))],
            out_specs=pl.BlockSpec((tq, hd), lambda b, h, i, j, nq=nq: (b * nq + i, h)),
            out_shape=jax.ShapeDtypeStruct((n_seq * s_len, DIFF_HEADS * hd), BF16),
            scratch_shapes=scratch,
            compiler_params=_cparams(("parallel", "parallel", "parallel", "arbitrary")),
            name="diff_attention_online" if online else "diff_attention",
        )(lam, sigmas, q, kt, v, subln_g.reshape(1, hd))
        outs.append(out)
        row0 += n_seq * s_len
    return jnp.concatenate(outs, axis=0)


def _diff_mixer(segs, z, qn_g, kn_g, lam, subln_g, lam_init):
    c_shift = (LOG2E * DIFF_DIM ** 0.5) * jnp.max(jnp.abs(qn_g)) * jnp.max(jnp.abs(kn_g))
    q, kt, v = _diff_prep(segs, z, qn_g, kn_g, c_shift.reshape(1).astype(F32))
    return lax.cond(c_shift < MAX_FIXED_SHIFT,
                    lambda: _diff_attention(segs, q, kt, v, lam, subln_g, lam_init, False),
                    lambda: _diff_attention(segs, q, kt, v, lam, subln_g, lam_init, True))


ROW_TILE = 8


def _load_rows(ref, n):
    return jnp.concatenate([ref[pl.ds(s, n, stride=ROW_TILE), :] for s in range(ROW_TILE)], axis=1)


def _store_rows(ref, x):
    n = x.shape[0]
    for s in range(ROW_TILE):
        ref[pl.ds(s, n, stride=ROW_TILE), :] = x[:, s * LANES:(s + 1) * LANES]


def _pack_bf16_pairs(x):
    n = x.shape[1] // 2
    lo = lax.bitcast_convert_type(x[:, :n].astype(BF16).astype(F32), U32) >> 16
    hi = lax.bitcast_convert_type(x[:, n:].astype(BF16).astype(F32), U32) & jnp.uint32(0xFFFF0000)
    return hi | lo


def _unpack_bf16_pairs(w):
    lo = lax.bitcast_convert_type(w << 16, F32)
    hi = lax.bitcast_convert_type(w & jnp.uint32(0xFFFF0000), F32)
    return lo, hi


def _router_kernel(x_ref, g_ref, sh_ref, sc_ref, wt_ref, rb_ref, tri_ref, ones_ref,
                   hp_ref, idx_ref, gate_ref, rank_ref, cnt_ref, run_sc):
    @pl.when(pl.program_id(0) == 0)
    def _():
        run_sc[...] = jnp.zeros_like(run_sc)

    h = _norm_modulate(x_ref[...], g_ref[...], sh_ref[...], sc_ref[...])
    _store_rows(hp_ref, _pack_bf16_pairs(h))
    lt = lax.dot_general(wt_ref[...], h.astype(BF16), (((1,), (1,)), ((), ())), preferred_element_type=F32)
    tm = lt.shape[1]
    scores = jax.nn.sigmoid(lt)
    biased = scores + rb_ref[...]
    neg_inf = -jnp.inf

    gsz = N_EXPERTS // N_GROUPS
    lio = lax.broadcasted_iota(I32, (gsz, tm), 0)
    grp = []
    for g in range(N_GROUPS):
        blk = biased[g * gsz:(g + 1) * gsz]
        m1 = jnp.max(blk, axis=0, keepdims=True)
        i1 = jnp.min(jnp.where(blk == m1, lio, gsz), axis=0, keepdims=True)
        m2 = jnp.max(jnp.where(lio == i1, neg_inf, blk), axis=0, keepdims=True)
        grp.append(m1 + m2)
    chosen = [jnp.zeros((1, tm), jnp.bool_) for _ in range(N_GROUPS)]
    for _ in range(TOPK_GROUPS):
        work = [jnp.where(chosen[g], neg_inf, grp[g]) for g in range(N_GROUPS)]
        mx = functools.reduce(jnp.maximum, work)
        found = jnp.zeros((1, tm), jnp.bool_)
        for g in range(N_GROUPS):
            take = (work[g] == mx) & jnp.logical_not(found)
            chosen[g] = chosen[g] | take
            found = found | take
    masked = jnp.concatenate(
        [jnp.where(chosen[g], biased[g * gsz:(g + 1) * gsz], neg_inf) for g in range(N_GROUPS)], axis=0)

    eio = lax.broadcasted_iota(I32, (N_EXPERTS, tm), 0)
    kio = lax.broadcasted_iota(I32, (TOP_K, tm), 0)
    picks, gates = [], []
    for _ in range(TOP_K):
        mx = jnp.max(masked, axis=0, keepdims=True)
        ik = jnp.min(jnp.where(masked == mx, eio, N_EXPERTS), axis=0, keepdims=True)
        hit = eio == ik
        gates.append(jnp.sum(jnp.where(hit, scores, 0.0), axis=0, keepdims=True))
        masked = jnp.where(hit, neg_inf, masked)
        picks.append(ik)
    inv = ROUTED_SCALE / functools.reduce(jnp.add, gates)

    hit_all = jnp.zeros((N_EXPERTS, tm), F32)
    for ik in picks:
        hit_all = hit_all + jnp.where(eio == ik, 1.0, 0.0)
    hit_b = hit_all.astype(BF16)
    before = jnp.dot(hit_b, tri_ref[...], preferred_element_type=F32)
    before = before + jnp.concatenate([run_sc[...]] * (tm // LANES), axis=1)
    idx_out = jnp.zeros((TOP_K, tm), I32)
    gate_out = jnp.zeros((TOP_K, tm), F32)
    rank_out = jnp.zeros((TOP_K, tm), F32)
    for k, ik in enumerate(picks):
        rk = jnp.sum(jnp.where(eio == ik, before, 0.0), axis=0, keepdims=True)
        idx_out = jnp.where(kio == k, ik, idx_out)
        gate_out = jnp.where(kio == k, gates[k] * inv, gate_out)
        rank_out = jnp.where(kio == k, rk, rank_out)
    idx_ref[...] = idx_out
    gate_ref[...] = gate_out
    rank_ref[...] = rank_out.astype(I32)
    run_sc[...] += jnp.dot(hit_b, ones_ref[...], preferred_element_type=F32)
    cnt_ref[...] = run_sc[...]


def _router(segs, x, norm, router_w, router_b):
    t, d = x.shape
    tm = min(ROUTE_ROWS, t)
    g, sh, sc = norm
    seq = lambda i: segs.seq_of_row(i * tm)
    tri = np.triu(np.ones((tm, tm), np.float32), 1)
    tok = lambda dt: jax.ShapeDtypeStruct((TOP_K, t), dt)
    tok_spec = pl.BlockSpec((TOP_K, tm), lambda i: (0, i))
    return pl.pallas_call(
        _router_kernel,
        grid=(t // tm,),
        in_specs=[pl.BlockSpec((tm, d), lambda i: (i, 0)),
                  pl.BlockSpec((1, d), lambda i: (0, 0)),
                  pl.BlockSpec((None, 1, d), lambda i: (seq(i), 0, 0)),
                  pl.BlockSpec((None, 1, d), lambda i: (seq(i), 0, 0)),
                  pl.BlockSpec((N_EXPERTS, d), lambda i: (0, 0)),
                  pl.BlockSpec((N_EXPERTS, 1), lambda i: (0, 0)),
                  pl.BlockSpec((tm, tm), lambda i: (0, 0)),
                  pl.BlockSpec((tm, LANES), lambda i: (0, 0))],
        out_specs=[pl.BlockSpec((tm * ROW_TILE, LANES), lambda i: (i, 0)), tok_spec, tok_spec, tok_spec,
                   pl.BlockSpec((N_EXPERTS, LANES), lambda i: (0, 0))],
        out_shape=[jax.ShapeDtypeStruct((t * ROW_TILE, LANES), U32), tok(I32), tok(F32), tok(I32),
                   jax.ShapeDtypeStruct((N_EXPERTS, LANES), F32)],
        scratch_shapes=[pltpu.VMEM((N_EXPERTS, LANES), F32)],
        compiler_params=_cparams(("arbitrary",)),
        name="moe_router",
    )(x, g.reshape(1, d), sh, sc, router_w.T.astype(BF16), router_b.astype(F32).reshape(N_EXPERTS, 1),
      jnp.asarray(tri, BF16), jnp.ones((tm, LANES), BF16))


def _row_tiles(ref, row, n=1):
    return ref.at[pl.ds(pl.multiple_of(row * ROW_TILE, ROW_TILE), n * ROW_TILE)]


def _row_copy(src_ref, src_row, dst_ref, dst_row, sem):
    return pltpu.make_async_copy(_row_tiles(src_ref, src_row), _row_tiles(dst_ref, dst_row), sem)


def _dest_kernel(idx_ref, rank_ref, start_ref, o_ref):
    tm = idx_ref.shape[1]
    eio = lax.broadcasted_iota(I32, (N_EXPERTS, tm), 0)
    for k in range(TOP_K):
        base = jnp.sum(jnp.where(eio == idx_ref[k:k + 1, :], start_ref[...], 0.0), axis=0, keepdims=True)
        o_ref[0, :, k * tm:(k + 1) * tm] = base.astype(I32) + rank_ref[k:k + 1, :]


def _dest_rows(idx, rank, pad_start, tm):
    t = idx.shape[1]
    tok_spec = pl.BlockSpec((TOP_K, tm), lambda i: (0, i))
    return pl.pallas_call(
        _dest_kernel,
        grid=(t // tm,),
        in_specs=[tok_spec, tok_spec, pl.BlockSpec((N_EXPERTS, 1), lambda i: (0, 0))],
        out_specs=pl.BlockSpec((1, 1, TOP_K * tm), lambda i: (i, 0, 0)),
        out_shape=jax.ShapeDtypeStruct((t // tm, 1, TOP_K * tm), I32),
        compiler_params=_cparams(("arbitrary",)),
        name="moe_dest_rows",
    )(idx, rank, pad_start.astype(F32).reshape(N_EXPERTS, 1))


def _dispatch_kernel(cend_ref, pend_ref, dest_ref, hp_ref, w1_ref, w3_ref, w2_ref, xs_ref, ysh_ref,
                     dest_sm, stage, zrow, sems, *, tm):
    i = pl.program_id(0)
    slot = i % 2
    n_dest = TOP_K * tm
    dest_cp = pltpu.make_async_copy(dest_ref.at[0, 0], dest_sm.at[pl.ds(slot * n_dest, n_dest)], sems.at[0])
    dest_cp.start()

    @pl.when(i == 0)
    def _():
        zrow[...] = jnp.zeros_like(zrow)

        def per_expert(e, carry):
            n_pad = pend_ref[e] - cend_ref[e]
            copies = []
            row = cend_ref[e]
            for b in reversed(range(MOE_ROWS.bit_length() - 1)):
                size = 1 << b
                bit = (n_pad >> b) & 1
                copies.append((bit, pltpu.make_async_copy(_row_tiles(zrow, 0, size), _row_tiles(xs_ref, row, size),
                                                          sems.at[1])))
                row = row + bit * size
            for bit, cp in copies:
                pl.when(bit == 1)(cp.start)
            for bit, cp in copies:
                pl.when(bit == 1)(cp.wait)
            return carry

        lax.fori_loop(0, N_EXPERTS, per_expert, 0)

    stage[slot] = hp_ref[...]
    dest_cp.wait()

    def row_copies(s, t):
        return [_row_copy(stage.at[s], t, xs_ref, dest_sm[s * n_dest + k * tm + t], sems.at[2 + s])
                for k in range(TOP_K)]

    def issue(s):
        def body(t, c):
            for cp in row_copies(s, t):
                cp.start()
            return c
        lax.fori_loop(0, tm, body, 0)

    def drain(s):
        def body(t, c):
            for cp in row_copies(s, t):
                cp.wait()
            return c
        lax.fori_loop(0, tm, body, 0)

    def finish(s):
        pl.when(i > 0)(lambda: drain(1 - s))
        pl.when(i == pl.num_programs(0) - 1)(lambda: drain(s))

    pl.when(slot == 0)(lambda: issue(0))
    pl.when(slot == 1)(lambda: issue(1))

    lo, hi = _unpack_bf16_pairs(_load_rows(hp_ref, tm))
    lo = lo.astype(BF16)
    hi = hi.astype(BF16)
    half = lo.shape[1]
    a = (jnp.dot(lo, w1_ref[:half], preferred_element_type=F32)
         + jnp.dot(hi, w1_ref[half:], preferred_element_type=F32))
    b = (jnp.dot(lo, w3_ref[:half], preferred_element_type=F32)
         + jnp.dot(hi, w3_ref[half:], preferred_element_type=F32))
    hb = (_silu(a) * b).astype(BF16)
    _store_rows(ysh_ref, _pack_bf16_pairs(jnp.dot(hb, w2_ref[...], preferred_element_type=F32)))

    pl.when(slot == 0)(lambda: finish(0))
    pl.when(slot == 1)(lambda: finish(1))


def _dispatch(hp, dest_tiles, cend, pend, n_rows, sw1, sw3, sw2):
    t = hp.shape[0] // ROW_TILE
    tm = dest_tiles.shape[-1] // TOP_K
    row_spec = pl.BlockSpec((tm * ROW_TILE, LANES), lambda i, ce, pe: (i, 0))
    full = lambda w: pl.BlockSpec(w.shape, lambda i, ce, pe: (0, 0))
    grid_spec = pltpu.PrefetchScalarGridSpec(
        num_scalar_prefetch=2,
        grid=(t // tm,),
        in_specs=[pl.BlockSpec((1, 1, TOP_K * tm), lambda i, ce, pe: (i, 0, 0)), row_spec,
                  full(sw1), full(sw3), full(sw2)],
        out_specs=[pl.BlockSpec(memory_space=pl.ANY), row_spec],
        scratch_shapes=[pltpu.SMEM((2 * TOP_K * tm,), I32), pltpu.VMEM((2, tm * ROW_TILE, LANES), U32),
                        pltpu.VMEM((MOE_ROWS // 2 * ROW_TILE, LANES), U32), pltpu.SemaphoreType.DMA((4,))],
    )
    return pl.pallas_call(
        functools.partial(_dispatch_kernel, tm=tm),
        grid_spec=grid_spec,
        out_shape=[jax.ShapeDtypeStruct((n_rows * ROW_TILE, LANES), U32),
                   jax.ShapeDtypeStruct((t * ROW_TILE, LANES), U32)],
        compiler_params=_cparams(("arbitrary",)),
        name="moe_dispatch",
    )(cend, pend, dest_tiles, hp, sw1.astype(BF16), sw3.astype(BF16), sw2.astype(BF16))


def _expert_kernel(be_ref, first_ref, used_ref, nxt_ref, slot_ref, x_ref, w1_hbm, w3_hbm, w2_hbm, o_ref,
                   w1_buf, w3_buf, w2_buf, w1_sc, w3_sc, w2_sc, sems, *, layer):
    i = pl.program_id(0)
    pairs = ((w1_hbm, w1_buf), (w3_hbm, w3_buf), (w2_hbm, w2_buf))

    def weight_copies(e, s):
        return [pltpu.make_async_copy(w.at[layer, e], b.at[s], sems.at[n, s]) for n, (w, b) in enumerate(pairs)]

    @pl.when(i == 0)
    def _():
        for cp in weight_copies(be_ref[0], slot_ref[0]):
            cp.start()

    @pl.when(first_ref[i] == 1)
    def _():
        s = slot_ref[i]
        for cp in weight_copies(be_ref[i], s):
            cp.wait()

        @pl.when(nxt_ref[i] >= 0)
        def _():
            for cp in weight_copies(nxt_ref[i], 1 - s):
                cp.start()

        w1_sc[...] = w1_buf[s].astype(BF16)
        w3_sc[...] = w3_buf[s].astype(BF16)
        w2_sc[...] = w2_buf[s].astype(BF16)

    @pl.when(i < used_ref[0])
    def _():
        lo, hi = _unpack_bf16_pairs(_load_rows(x_ref, MOE_ROWS))
        lo = lo.astype(BF16)
        hi = hi.astype(BF16)
        half = lo.shape[1]
        a = (jnp.dot(lo, w1_sc[:half], preferred_element_type=F32)
             + jnp.dot(hi, w1_sc[half:], preferred_element_type=F32))
        b = (jnp.dot(lo, w3_sc[:half], preferred_element_type=F32)
             + jnp.dot(hi, w3_sc[half:], preferred_element_type=F32))
        hb = (_silu(a) * b).astype(BF16)
        _store_rows(o_ref, _pack_bf16_pairs(jnp.dot(hb, w2_sc[...], preferred_element_type=F32)))

    @pl.when(i >= used_ref[0])
    def _():
        o_ref[...] = jnp.zeros_like(o_ref)


def _experts(xs, block_e, first, used, w1, w3, w2, layer):
    rows = xs.shape[0] // ROW_TILE
    d = 2 * ROW_TILE * LANES
    assert w1.shape[-2] == d, "token rows are stored as one (8, 128) word tile each"
    tm = MOE_ROWS
    n_blocks = rows // tm
    de = w1.shape[-1]
    row_spec = pl.BlockSpec((tm * ROW_TILE, LANES), lambda i, *_: (i, 0))
    n_e = w1.shape[1]
    eids = jnp.arange(n_e, dtype=I32)
    owns = jnp.any(block_e[:, None] == eids[None, :], axis=0)
    later = jnp.where(owns[None, :] & (eids[None, :] > block_e[:, None]), eids[None, :], n_e)
    nxt = jnp.min(later, axis=1)
    nxt = jnp.where(nxt < n_e, nxt, -1).astype(I32)
    slot = ((jnp.cumsum(first) - 1) % 2).astype(I32)
    any_spec = pl.BlockSpec(memory_space=pl.ANY)
    grid_spec = pltpu.PrefetchScalarGridSpec(
        num_scalar_prefetch=5,
        grid=(n_blocks,),
        in_specs=[row_spec, any_spec, any_spec, any_spec],
        out_specs=row_spec,
        scratch_shapes=[pltpu.VMEM((2, d, de), F32), pltpu.VMEM((2, d, de), F32), pltpu.VMEM((2, de, d), F32),
                        pltpu.VMEM((d, de), BF16), pltpu.VMEM((d, de), BF16), pltpu.VMEM((de, d), BF16),
                        pltpu.SemaphoreType.DMA((3, 2))],
    )
    return pl.pallas_call(
        functools.partial(_expert_kernel, layer=layer),
        grid_spec=grid_spec,
        out_shape=jax.ShapeDtypeStruct((rows * ROW_TILE, LANES), U32),
        compiler_params=_cparams(("arbitrary",)),
        name="swiglu_experts",
    )(block_e, first, used, nxt, slot, xs, w1, w3, w2)


def _block_plan(counts, tm, n_blocks):
    padded = (counts + tm - 1) // tm * tm
    pad_end = jnp.cumsum(padded)
    pad_start = pad_end - padded
    blk_row = jnp.arange(n_blocks, dtype=I32) * tm
    block_e = jnp.minimum(jnp.sum((pad_end[None, :] <= blk_row[:, None]).astype(I32), axis=1), N_EXPERTS - 1)
    first = jnp.concatenate([jnp.ones((1,), I32), (block_e[1:] != block_e[:-1]).astype(I32)])
    used = (pad_end[-1:] // tm).astype(I32)
    return pad_start.astype(I32), pad_end.astype(I32), block_e, first, used


def _combine_kernel(dcur_ref, dnxt_ref, x_ref, gate_ref, ysh_ref, g_ref, ys_ref, o_ref, dest_sm, buf, sems,
                    *, tm):
    i = pl.program_id(0)
    slot = i % 2
    n_dest = TOP_K * tm

    def row_copies(s, t):
        return [_row_copy(ys_ref, dest_sm[s * n_dest + k * tm + t], buf.at[s, k], t, sems.at[1 + s])
                for k in range(TOP_K)]

    def gather(dref, s):
        dest_cp = pltpu.make_async_copy(dref.at[0, 0], dest_sm.at[pl.ds(s * n_dest, n_dest)], sems.at[0])
        dest_cp.start()
        dest_cp.wait()

        def issue(t, c):
            for cp in row_copies(s, t):
                cp.start()
            return c

        lax.fori_loop(0, tm, issue, 0)

    def drain(s):
        def body(t, c):
            for cp in row_copies(s, t):
                cp.wait()
            return c
        lax.fori_loop(0, tm, body, 0)

    def fetch(s):
        pl.when(i + 1 < pl.num_programs(0))(lambda: gather(dnxt_ref, 1 - s))
        drain(s)

    pl.when(i == 0)(lambda: gather(dcur_ref, 0))
    pl.when(slot == 0)(lambda: fetch(0))
    pl.when(slot == 1)(lambda: fetch(1))

    lo, hi = _unpack_bf16_pairs(_load_rows(ysh_ref, tm))
    for k in range(TOP_K):
        gk = g_ref[:, k:k + 1]
        rl, rh = _unpack_bf16_pairs(_load_rows(buf.at[slot, k], tm))
        lo = lo + gk * rl
        hi = hi + gk * rh
    half = lo.shape[1]
    o_ref[:, :half] = x_ref[:, :half] + gate_ref[:, :half] * lo
    o_ref[:, half:] = x_ref[:, half:] + gate_ref[:, half:] * hi


def _combine(segs, x, gate, ysh, ys, g_tok, dest_tiles):
    t, d = x.shape
    half = d // 2
    tm = dest_tiles.shape[-1] // TOP_K
    seq = lambda i: segs.seq_of_row(i * tm)
    n_tiles = t // tm
    return pl.pallas_call(
        functools.partial(_combine_kernel, tm=tm),
        grid=(n_tiles,),
        in_specs=[pl.BlockSpec((1, 1, TOP_K * tm), lambda i: (i, 0, 0)),
                  pl.BlockSpec((1, 1, TOP_K * tm), lambda i: (jnp.minimum(i + 1, n_tiles - 1), 0, 0)),
                  pl.BlockSpec((tm, d), lambda i: (i, 0)),
                  pl.BlockSpec((None, 1, d), lambda i: (seq(i), 0, 0)),
                  pl.BlockSpec((tm * ROW_TILE, LANES), lambda i: (i, 0)),
                  pl.BlockSpec((tm, TOP_K), lambda i: (i, 0)),
                  pl.BlockSpec(memory_space=pl.ANY)],
        out_specs=pl.BlockSpec((tm, d), lambda i: (i, 0)),
        out_shape=jax.ShapeDtypeStruct((t, d), F32),
        scratch_shapes=[pltpu.SMEM((2 * TOP_K * tm,), I32), pltpu.VMEM((2, TOP_K, tm * ROW_TILE, LANES), U32),
                        pltpu.SemaphoreType.DMA((3,))],
        compiler_params=_cparams(("arbitrary",)),
        name="moe_combine",
    )(dest_tiles, dest_tiles, x, gate, ysh, g_tok, ys)


def _moe_layer(segs, x, norm, gate, router_w, router_b, exp_w1, exp_w3, exp_w2, sh_w1, sh_w3, sh_w2, layer):
    t, d = x.shape
    hp, idx, g, rank, cnt = _router(segs, x, norm, router_w, router_b)
    counts = cnt[:, 0].astype(I32)
    n_blocks = (t * TOP_K) // MOE_ROWS + N_EXPERTS
    pad_start, pad_end, block_e, first, used = _block_plan(counts, MOE_ROWS, n_blocks)
    dest_tiles = _dest_rows(idx, rank, pad_start, min(ROUTE_ROWS, t))
    xs, ysh = _dispatch(hp, dest_tiles, pad_start + counts, pad_end, n_blocks * MOE_ROWS,
                        sh_w1[layer], sh_w3[layer], sh_w2[layer])
    ys = _experts(xs, block_e, first, used, exp_w1, exp_w3, exp_w2, layer)
    return _combine(segs, x, gate, ysh, ys, g.T, dest_tiles)


def kernel(x_prompt, x_sample, c_prompt, c_sample, w_ada, b_ada, norm1_g, norm2_g, w_in_e, gm_ln_g, gm_ln_b,
           gm_ws, gm_bs, swa_qn_g, swa_kn_g, swa_sink, w_out_e, w_in_o, diff_qn_g, diff_kn_g, lam_q1, lam_k1,
           lam_q2, lam_k2, diff_subln_g, w_out_o, router_w, router_b, exp_w1, exp_w3, exp_w2, sh_w1, sh_w3,
           sh_w2):
    d = x_prompt.shape[-1]
    depth = w_ada.shape[0]
    segs = _Segs([x_prompt.shape[:2], x_sample.shape[:2]])
    n_seq = segs.n_seq
    x = jnp.concatenate([x_prompt.reshape(-1, d), x_sample.reshape(-1, d)], axis=0)

    c_rows = -(-n_seq // 16) * 16
    c_all = jnp.zeros((c_rows, d), F32).at[:n_seq].set(jnp.concatenate([c_prompt, c_sample], axis=0))
    mod = _ada(c_all, w_ada, b_ada)[:, :n_seq]

    for l in range(depth):
        sh1, sc1, g1, sh2, sc2, g2 = [m.reshape(n_seq, 1, d) for m in jnp.split(mod[l], 6, axis=-1)]
        j = l // 2
        if l % 2 == 0:
            z = _matmul(segs, x, w_in_e[j].astype(BF16), norm=(norm1_g[l], sh1, sc1))
            a = _gmlp(z, gm_ln_g[j], gm_ln_b[j], gm_ws[j], gm_bs[j])
            b = _swa(segs, z, swa_qn_g[j], swa_kn_g[j], swa_sink[j])
            mix = jnp.concatenate([a, b], axis=-1)
            x = _matmul(segs, mix, w_out_e[j].astype(BF16), resid=(x, g1))
        else:
            lam_init = 0.8 - 0.6 * math.exp(-0.3 * l)
            z = _matmul(segs, x, w_in_o[j].astype(BF16), norm=(norm1_g[l], sh1, sc1))
            lam = (jnp.exp(jnp.sum(lam_q1[j].astype(F32) * lam_k1[j].astype(F32)))
                   - jnp.exp(jnp.sum(lam_q2[j].astype(F32) * lam_k2[j].astype(F32))) + lam_init).reshape(1)
            o = _diff_mixer(segs, z, diff_qn_g[j], diff_kn_g[j], lam, diff_subln_g[j], lam_init)
            x = _matmul(segs, o, w_out_o[j].astype(BF16), resid=(x, g1))
        x = _moe_layer(segs, x, (norm2_g[l], sh2, sc2), g2, router_w[l], router_b[l], exp_w1, exp_w3, exp_w2,
                       sh_w1, sh_w3, sh_w2, l)

    t0 = x_prompt.shape[0] * x_prompt.shape[1]
    return x[:t0].reshape(x_prompt.shape), x[t0:].reshape(x_sample.shape)
```
